```python
import jax, jax.numpy as jnp
from jax import lax
import numpy as np

D_MODEL = 1024
BATCH = 8
SEQ = 2048
DEPTH = 4

N_MIXERS = 3
N_SUB = 3
FFN_RES_WEIGHT = 0.5
D_FF = 2816
RMS_EPS = 1e-6
FOX_HEADS = 16
FOX_HEAD_DIM = D_MODEL // FOX_HEADS
FOX_BLOCK = 128
SCONV_WIDTH = 3
LRU_WIDTH = D_MODEL
LRU_BLOCKS = 16
LRU_BLOCK_DIM = LRU_WIDTH // LRU_BLOCKS
LRU_CONV_WIDTH = 4
LRU_C = 8.0
N_FOX = len(range(0, DEPTH, N_MIXERS))
N_SCONV = len(range(1, DEPTH, N_MIXERS))
N_LRU = len(range(2, DEPTH, N_MIXERS))

kernel_name = "hybrid_fox_shortconv_rglru_macaron"


def rmsnorm(x, g):
    x32 = x.astype(jnp.float32)
    y = x32 * lax.rsqrt(jnp.mean(x32 * x32, axis=-1, keepdims=True) + RMS_EPS)
    return y.astype(x.dtype) * g


def causal_depthwise_conv(u, w, b=None):
    k_w, ch = w.shape
    out = lax.conv_general_dilated(
        u, w[:, None, :].astype(u.dtype), window_strides=(1,),
        padding=[(k_w - 1, 0)], dimension_numbers=("NWC", "WIO", "NWC"),
        feature_group_count=ch)
    if b is not None:
        out = out + b
    return out


def swiglu(h, w_in, w_out):
    g, u = jnp.split(h @ w_in, 2, axis=-1)
    return (jax.nn.silu(g) * u) @ w_out


def fox_mixer(h, w_in, b_f, w_out):
    bsz, seq, _ = h.shape
    proj = h @ w_in
    q, k, v, f_logit = jnp.split(proj, [D_MODEL, 2 * D_MODEL, 3 * D_MODEL], axis=-1)
    q = q.reshape(bsz, seq, FOX_HEADS, FOX_HEAD_DIM)
    k = k.reshape(bsz, seq, FOX_HEADS, FOX_HEAD_DIM)
    v = v.reshape(bsz, seq, FOX_HEADS, FOX_HEAD_DIM)
    log_f = jax.nn.log_sigmoid((f_logit + b_f).astype(jnp.float32))
    cum = jnp.cumsum(log_f, axis=1).transpose(0, 2, 1)
    scale = FOX_HEAD_DIM ** -0.5
    outs = []
    for blk in range(seq // FOX_BLOCK):
        s0 = blk * FOX_BLOCK
        s1 = s0 + FOX_BLOCK
        logits = jnp.einsum("bqhd,bkhd->bhqk", q[:, s0:s1], k[:, :s1]).astype(jnp.float32) * scale
        logits = logits + cum[:, :, s0:s1, None] - cum[:, :, None, :s1]
        q_pos = jnp.arange(s0, s1)[:, None]
        k_pos = jnp.arange(s1)[None, :]
        logits = jnp.where(k_pos <= q_pos, logits, -jnp.inf)
        p = jax.nn.softmax(logits, axis=-1).astype(v.dtype)
        outs.append(jnp.einsum("bhqk,bkhd->bqhd", p, v[:, :s1]))
    o = jnp.concatenate(outs, axis=1).reshape(bsz, seq, D_MODEL)
    return o @ w_out


def sconv_mixer(h, w_in, conv_w, w_out):
    b_gate, c_gate, xv = jnp.split(h @ w_in, 3, axis=-1)
    y = b_gate * causal_depthwise_conv(c_gate * xv, conv_w)
    return y @ w_out


def lru_mixer(h, w_in, conv_w, conv_b, w_a, b_a, w_x, b_x, lam, w_out):
    bsz, seq, _ = h.shape
    gate, xb = jnp.split(h @ w_in, 2, axis=-1)
    xb = causal_depthwise_conv(xb, conv_w, conv_b)
    xh = xb.reshape(bsz, seq, LRU_BLOCKS, LRU_BLOCK_DIM)
    r = jax.nn.sigmoid(jnp.einsum("bsni,nij->bsnj", xh, w_a) + b_a).reshape(bsz, seq, LRU_WIDTH)
    i = jax.nn.sigmoid(jnp.einsum("bsni,nij->bsnj", xh, w_x) + b_x).reshape(bsz, seq, LRU_WIDTH)
    log_a = -LRU_C * r.astype(jnp.float32) * jax.nn.softplus(-lam.astype(jnp.float32))
    a = jnp.exp(log_a)
    mult = jnp.sqrt(-jnp.expm1(2.0 * log_a))
    b_term = mult * (i * xb).astype(jnp.float32)

    def combine(left, right):
        a1, b1 = left
        a2, b2 = right
        return a1 * a2, a2 * b1 + b2

    _, hs = lax.associative_scan(combine, (a, b_term), axis=1)
    y = hs.astype(h.dtype) * jax.nn.gelu(gate)
    return y @ w_out


def setup_inputs(seed: int = 0) -> dict:
    key = jax.random.key(seed)
    ks = jax.random.split(key, 24)
    f32 = jnp.float32

    def nrm(k, shape, fan_in):
        return jax.random.normal(k, shape, f32) * (fan_in ** -0.5)

    x = jax.random.normal(ks[0], (BATCH, SEQ, D_MODEL), f32)
    c = jax.random.normal(ks[1], (BATCH, D_MODEL), f32)
    w_cond = nrm(ks[2], (DEPTH, D_MODEL, N_SUB * 3 * D_MODEL), D_MODEL)
    b_cond = 0.02 * jax.random.normal(ks[3], (DEPTH, N_SUB * 3 * D_MODEL), f32)
    norm_pre = 1.0 + 0.05 * jax.random.normal(ks[4], (DEPTH, N_SUB, D_MODEL), f32)
    norm_post = 1.0 + 0.05 * jax.random.normal(ks[5], (DEPTH, N_SUB, D_MODEL), f32)
    w_ffn_in = nrm(ks[6], (DEPTH, 2, D_MODEL, 2 * D_FF), D_MODEL)
    w_ffn_out = nrm(ks[7], (DEPTH, 2, D_FF, D_MODEL), D_FF)
    fox_w_in = nrm(ks[8], (N_FOX, D_MODEL, 3 * D_MODEL + FOX_HEADS), D_MODEL)
    fox_b_f = jax.random.uniform(ks[9], (N_FOX, FOX_HEADS), f32, 1.0, 4.0)
    fox_w_out = nrm(ks[10], (N_FOX, D_MODEL, D_MODEL), D_MODEL)
    sconv_w_in = nrm(ks[11], (N_SCONV, D_MODEL, 3 * D_MODEL), D_MODEL)
    sconv_conv_w = nrm(ks[12], (N_SCONV, SCONV_WIDTH, D_MODEL), SCONV_WIDTH)
    sconv_w_out = nrm(ks[13], (N_SCONV, D_MODEL, D_MODEL), D_MODEL)
    lru_w_in = nrm(ks[14], (N_LRU, D_MODEL, 2 * LRU_WIDTH), D_MODEL)
    lru_conv_w = nrm(ks[15], (N_LRU, LRU_CONV_WIDTH, LRU_WIDTH), LRU_CONV_WIDTH)
    lru_conv_b = 0.02 * jax.random.normal(ks[16], (N_LRU, LRU_WIDTH), f32)
    lru_w_a = nrm(ks[17], (N_LRU, LRU_BLOCKS, LRU_BLOCK_DIM, LRU_BLOCK_DIM), LRU_BLOCK_DIM)
    lru_b_a = 0.02 * jax.random.normal(ks[18], (N_LRU, LRU_BLOCKS, LRU_BLOCK_DIM), f32)
    lru_w_x = nrm(ks[19], (N_LRU, LRU_BLOCKS, LRU_BLOCK_DIM, LRU_BLOCK_DIM), LRU_BLOCK_DIM)
    lru_b_x = 0.02 * jax.random.normal(ks[20], (N_LRU, LRU_BLOCKS, LRU_BLOCK_DIM), f32)
    a_c = jax.random.uniform(ks[21], (N_LRU, LRU_WIDTH), f32, 0.9, 0.999)
    s = a_c ** (1.0 / LRU_C)
    lru_lambda = jnp.log(s) - jnp.log1p(-s)
    lru_w_out = nrm(ks[22], (N_LRU, LRU_WIDTH, D_MODEL), LRU_WIDTH)
    return {
        "x": x, "c": c, "w_cond": w_cond, "b_cond": b_cond,
        "norm_pre": norm_pre, "norm_post": norm_post,
        "w_ffn_in": w_ffn_in, "w_ffn_out": w_ffn_out,
        "fox_w_in": fox_w_in, "fox_b_f": fox_b_f, "fox_w_out": fox_w_out,
        "sconv_w_in": sconv_w_in, "sconv_conv_w": sconv_conv_w, "sconv_w_out": sconv_w_out,
        "lru_w_in": lru_w_in, "lru_conv_w": lru_conv_w, "lru_conv_b": lru_conv_b,
        "lru_w_a": lru_w_a, "lru_b_a": lru_b_a, "lru_w_x": lru_w_x, "lru_b_x": lru_b_x,
        "lru_lambda": lru_lambda, "lru_w_out": lru_w_out,
    }


def reference(x, c, w_cond, b_cond, norm_pre, norm_post, w_ffn_in, w_ffn_out,
              fox_w_in, fox_b_f, fox_w_out, sconv_w_in, sconv_conv_w, sconv_w_out,
              lru_w_in, lru_conv_w, lru_conv_b, lru_w_a, lru_b_a, lru_w_x, lru_b_x,
              lru_lambda, lru_w_out):
    bsz = x.shape[0]
    c_act = jax.nn.silu(c)
    for i in range(DEPTH):
        mod = (c_act @ w_cond[i] + b_cond[i]).reshape(bsz, N_SUB, 3, D_MODEL)
        shift = mod[:, :, 0, None, :]
        scale = mod[:, :, 1, None, :]
        gate = mod[:, :, 2, None, :]

        def pre(h, s):
            return rmsnorm(h, norm_pre[i, s]) * (1.0 + scale[:, s]) + shift[:, s]

        y = swiglu(pre(x, 0), w_ffn_in[i, 0], w_ffn_out[i, 0])
        x = x + FFN_RES_WEIGHT * gate[:, 0] * rmsnorm(y, norm_post[i, 0])

        h = pre(x, 1)
        kind = i % N_MIXERS
        j = i // N_MIXERS
        if kind == 0:
            y = fox_mixer(h, fox_w_in[j], fox_b_f[j], fox_w_out[j])
        elif kind == 1:
            y = sconv_mixer(h, sconv_w_in[j], sconv_conv_w[j], sconv_w_out[j])
        else:
            y = lru_mixer(h, lru_w_in[j], lru_conv_w[j], lru_conv_b[j], lru_w_a[j], lru_b_a[j],
                          lru_w_x[j], lru_b_x[j], lru_lambda[j], lru_w_out[j])
        x = x + gate[:, 1] * rmsnorm(y, norm_post[i, 1])

        y = swiglu(pre(x, 2), w_ffn_in[i, 1], w_ffn_out[i, 1])
        x = x + FFN_RES_WEIGHT * gate[:, 2] * rmsnorm(y, norm_post[i, 2])
    return x
```

```python
import functools

import numpy as np
import jax
import jax.numpy as jnp
from jax import lax
from jax.experimental import pallas as pl
from jax.experimental.pallas import tpu as pltpu

N_SUB = 3
N_MIXERS = 3
FFN_RES_WEIGHT = 0.5
RMS_EPS = 1e-6
FOX_HEAD_DIM = 64
LRU_C = 8.0

V7X_LANES = 128
V7X_SUBLANES = 8
V7X_MXU_DIM = 256
V7X_VMEM_BYTES = 64 * 1024 * 1024
VMEM_LIMIT_BYTES = V7X_VMEM_BYTES - 8 * 1024 * 1024

ROW_TILE = 512
ATTN_TILE = 256
FFN_CHUNK = V7X_MXU_DIM
AUG_PER_HEAD = 6
MOD_COL_TILE = 1536

BF16 = jnp.bfloat16
F32 = jnp.float32


def _params(n_axes):
    return pltpu.CompilerParams(dimension_semantics=("arbitrary",) * n_axes,
                                vmem_limit_bytes=VMEM_LIMIT_BYTES)


def _resident(shape):
    return pl.BlockSpec(shape, lambda *_: (0,) * len(shape), pipeline_mode=pl.Buffered(1))


def _rms(x):
    return x * lax.rsqrt(jnp.mean(x * x, axis=-1, keepdims=True) + RMS_EPS)


def _pre(x, gpre_ref, scale_ref, shift_ref):
    return (_rms(x) * gpre_ref[...]) * (1.0 + scale_ref[0]) + shift_ref[0]


def _split3(x):
    hi = x.astype(BF16)
    r1 = x - hi.astype(F32)
    mid = r1.astype(BF16)
    lo = (r1 - mid.astype(F32)).astype(BF16)
    return hi, mid, lo


def _mod_kernel(c_ref, w_ref, b_ref, o_ref):
    c = c_ref[...]
    c_act = (c * jax.nn.sigmoid(c)).astype(BF16)
    o_ref[0] = jnp.dot(c_act, w_ref[0].astype(BF16), preferred_element_type=F32) + b_ref[0]


def _modulation(c, w_cond, b_cond):
    depth, d, n = w_cond.shape
    bsz = c.shape[0]
    tn = MOD_COL_TILE if n % MOD_COL_TILE == 0 else n
    return pl.pallas_call(
        _mod_kernel,
        grid=(depth, n // tn),
        in_specs=[
            pl.BlockSpec((bsz, d), lambda i, j: (0, 0)),
            pl.BlockSpec((1, d, tn), lambda i, j: (i, 0, j)),
            pl.BlockSpec((1, 1, tn), lambda i, j: (i, 0, j)),
        ],
        out_specs=pl.BlockSpec((1, bsz, tn), lambda i, j: (i, 0, j)),
        out_shape=jax.ShapeDtypeStruct((depth, bsz, n), F32),
        compiler_params=_params(2),
        name="adaln_mod",
    )(c, w_cond, b_cond.reshape(depth, 1, n))


def _ffn_kernel(x_ref, shift_ref, scale_ref, gate_ref, gpre_ref, gpost_ref, win_ref, wout_ref,
                o_ref, act_ref):
    x = x_ref[...]
    h = _pre(x, gpre_ref, scale_ref, shift_ref).astype(BF16)
    n_chunks, _, two_fc = win_ref.shape
    fc = two_fc // 2
    for c in range(n_chunks):
        gu = jnp.dot(h, win_ref[c], preferred_element_type=F32)
        g = gu[:, :fc]
        u = gu[:, fc:]
        act_ref[:, c * fc:(c + 1) * fc] = (g * jax.nn.sigmoid(g) * u).astype(BF16)
    y = jnp.dot(act_ref[...], wout_ref[...], preferred_element_type=F32)
    o_ref[...] = x + (FFN_RES_WEIGHT * gate_ref[0]) * (_rms(y) * gpost_ref[...])


def _ffn_weights(w_in, w_out):
    d, two_f = w_in.shape
    f = two_f // 2
    fc = FFN_CHUNK
    assert f % fc == 0
    w = w_in.astype(BF16).reshape(d, 2, f // fc, fc)
    w = jnp.transpose(w, (2, 0, 1, 3)).reshape(f // fc, d, 2 * fc)
    return w, w_out.astype(BF16)


def _row_specs(tm, d, tiles_per_batch):
    row = pl.BlockSpec((tm, d), lambda i: (i, 0))
    mod = pl.BlockSpec((1, 1, d), lambda i: (i // tiles_per_batch, 0, 0))
    vec = pl.BlockSpec((1, d), lambda i: (0, 0))
    return row, mod, vec


def _ffn(x2, seq, shift, scale, gate, g_pre, g_post, w_in_c, w_out_c):
    t, d = x2.shape
    tm = min(ROW_TILE, seq)
    f = w_out_c.shape[0]
    row, mod, vec = _row_specs(tm, d, seq // tm)
    return pl.pallas_call(
        _ffn_kernel,
        grid=(t // tm,),
        in_specs=[row, mod, mod, mod, vec, vec, _resident(w_in_c.shape), _resident(w_out_c.shape)],
        out_specs=row,
        out_shape=jax.ShapeDtypeStruct((t, d), F32),
        scratch_shapes=[pltpu.VMEM((tm, f), BF16)],
        compiler_params=_params(1),
        name="ffn",
    )(x2, shift, scale, gate, g_pre, g_post, w_in_c, w_out_c)


def _shift_rows(u, prev, k):
    rows = lax.broadcasted_iota(jnp.int32, u.shape, 0)
    out = pltpu.roll(u, k, axis=0)
    for j in range(k):
        out = jnp.where(rows == j, prev[V7X_SUBLANES - k + j:V7X_SUBLANES - k + j + 1], out)
    return out


def _causal_conv(u, prev, w_ref):
    kw = w_ref.shape[0]
    out = w_ref[kw - 1:kw] * u
    for k in range(1, kw):
        out = out + w_ref[kw - 1 - k:kw - k] * _shift_rows(u, prev, k)
    return out


def _sconv_kernel(tiles_per_batch, x_ref, shift_ref, scale_ref, gate_ref, gpre_ref, gpost_ref,
                  win_ref, convw_ref, wout_ref, o_ref, carry_ref):
    d = x_ref.shape[1]
    tm = x_ref.shape[0]

    @pl.when(pl.program_id(0) % tiles_per_batch == 0)
    def _():
        carry_ref[...] = jnp.zeros(carry_ref.shape, F32)

    x = x_ref[...]
    h = _pre(x, gpre_ref, scale_ref, shift_ref).astype(BF16)
    c_gate = jnp.dot(h, win_ref[:, d:2 * d], preferred_element_type=F32)
    xv = jnp.dot(h, win_ref[:, 2 * d:], preferred_element_type=F32)
    u = c_gate * xv
    conv = _causal_conv(u, carry_ref[...], convw_ref)
    carry_ref[...] = u[tm - V7X_SUBLANES:]
    b_gate = jnp.dot(h, win_ref[:, :d], preferred_element_type=F32)
    y = jnp.dot((b_gate * conv).astype(BF16), wout_ref[...], preferred_element_type=F32)
    o_ref[...] = x + gate_ref[0] * (_rms(y) * gpost_ref[...])


def _sconv(x2, seq, shift, scale, gate, g_pre, g_post, w_in, conv_w, w_out):
    t, d = x2.shape
    tm = min(ROW_TILE, seq)
    row, mod, vec = _row_specs(tm, d, seq // tm)
    return pl.pallas_call(
        functools.partial(_sconv_kernel, seq // tm),
        grid=(t // tm,),
        in_specs=[row, mod, mod, mod, vec, vec, _resident(w_in.shape), _resident(conv_w.shape),
                  _resident(w_out.shape)],
        out_specs=row,
        out_shape=jax.ShapeDtypeStruct((t, d), F32),
        scratch_shapes=[pltpu.VMEM((V7X_SUBLANES, d), F32)],
        compiler_params=_params(1),
        name="sconv_mixer",
    )(x2, shift, scale, gate, g_pre, g_post, w_in.astype(BF16), conv_w, w_out.astype(BF16))


def _linear_scan(a, b):
    n = a.shape[0]
    rows = lax.broadcasted_iota(jnp.int32, a.shape, 0)
    d = 1
    while d < n:
        keep = rows >= d
        b = jnp.where(keep, b + a * pltpu.roll(b, d, axis=0), b)
        a = jnp.where(keep, a * pltpu.roll(a, d, axis=0), a)
        d *= 2
    return a, b


def _lru_kernel(tiles_per_batch, x_ref, shift_ref, scale_ref, gate_ref, gpre_ref, gpost_ref,
                win_ref, convw_ref, convb_ref, wgate_ref, ba_ref, bx_ref, lam_ref, wout_ref,
                o_ref, xcarry_ref, hcarry_ref):
    tm = x_ref.shape[0]
    w = wout_ref.shape[0]
    n_groups, gk, two_gn = wgate_ref.shape
    gn = two_gn // 2

    @pl.when(pl.program_id(0) % tiles_per_batch == 0)
    def _():
        xcarry_ref[...] = jnp.zeros(xcarry_ref.shape, F32)
        hcarry_ref[...] = jnp.zeros(hcarry_ref.shape, F32)

    x = x_ref[...]
    h = _pre(x, gpre_ref, scale_ref, shift_ref).astype(BF16)
    xraw = jnp.dot(h, win_ref[:, w:], preferred_element_type=F32)
    xb = _causal_conv(xraw, xcarry_ref[...], convw_ref) + convb_ref[...]
    xcarry_ref[...] = xraw[tm - V7X_SUBLANES:]

    xb16 = xb.astype(BF16)
    r_parts, i_parts = [], []
    for g in range(n_groups):
        ri = jnp.dot(xb16[:, g * gk:(g + 1) * gk], wgate_ref[g], preferred_element_type=F32)
        r_parts.append(ri[:, :gn])
        i_parts.append(ri[:, gn:])
    r = jax.nn.sigmoid(jnp.concatenate(r_parts, axis=1) + ba_ref[...])
    i = jax.nn.sigmoid(jnp.concatenate(i_parts, axis=1) + bx_ref[...])

    neg_lam = -lam_ref[...]
    softplus = jnp.maximum(neg_lam, 0.0) + jnp.log1p(jnp.exp(-jnp.abs(neg_lam)))
    log_a = (-LRU_C * softplus) * r
    a = jnp.exp(log_a)
    mult = jnp.sqrt(-jnp.tanh(log_a) * (1.0 + a * a))
    a_cum, b_cum = _linear_scan(a, mult * (i * xb))
    hs = a_cum * hcarry_ref[V7X_SUBLANES - 1:] + b_cum
    hcarry_ref[...] = hs[tm - V7X_SUBLANES:]

    gate_br = jnp.dot(h, win_ref[:, :w], preferred_element_type=F32)
    y = jnp.dot((hs * jax.nn.gelu(gate_br)).astype(BF16), wout_ref[...], preferred_element_type=F32)
    o_ref[...] = x + gate_ref[0] * (_rms(y) * gpost_ref[...])


def _lru_gate_weights(w_a, w_x):
    nb, bd, _ = w_a.shape
    per = V7X_MXU_DIM // bd
    assert nb % per == 0

    def dense(wb):
        wg = wb.reshape(nb // per, per, bd, bd)
        eye = jnp.eye(per, dtype=wb.dtype)
        return jnp.einsum("gpij,pq->gpiqj", wg, eye).reshape(nb // per, per * bd, per * bd)

    return jnp.concatenate([dense(w_a), dense(w_x)], axis=-1).astype(BF16)


def _lru(x2, seq, shift, scale, gate, g_pre, g_post, w_in, conv_w, conv_b, w_a, b_a, w_x, b_x, lam,
         w_out):
    t, d = x2.shape
    w = w_out.shape[0]
    tm = min(ROW_TILE, seq)
    row, mod, vec = _row_specs(tm, d, seq // tm)
    wvec = pl.BlockSpec((1, w), lambda i: (0, 0))
    w_gate = _lru_gate_weights(w_a, w_x)
    return pl.pallas_call(
        functools.partial(_lru_kernel, seq // tm),
        grid=(t // tm,),
        in_specs=[row, mod, mod, mod, vec, vec, _resident(w_in.shape), _resident(conv_w.shape), wvec,
                  _resident(w_gate.shape), wvec, wvec, wvec, _resident(w_out.shape)],
        out_specs=row,
        out_shape=jax.ShapeDtypeStruct((t, d), F32),
        scratch_shapes=[pltpu.VMEM((V7X_SUBLANES, w), F32), pltpu.VMEM((V7X_SUBLANES, w), F32)],
        compiler_params=_params(1),
        name="lru_mixer",
    )(x2, shift, scale, gate, g_pre, g_post, w_in.astype(BF16), conv_w, conv_b.reshape(1, w), w_gate,
      b_a.reshape(1, w), b_x.reshape(1, w), lam.reshape(1, w), w_out.astype(BF16))


def _fox_proj_kernel(tiles_per_batch, n_heads, x_ref, shift_ref, scale_ref, gpre_ref, wqkv_ref, wf_ref, bf_ref,
                     tri_ref, pq_ref, pk_ref, oneq_ref, onek_ref,
                     q_ref, k_ref, v_ref, augq_ref, augk_ref, carry_ref):
    d = x_ref.shape[1]
    n_pairs = q_ref.shape[1]

    @pl.when(pl.program_id(0) % tiles_per_batch == 0)
    def _():
        carry_ref[...] = jnp.zeros(carry_ref.shape, F32)

    h = _pre(x_ref[...], gpre_ref, scale_ref, shift_ref).astype(BF16)
    q = jnp.dot(h, wqkv_ref[:, :d], preferred_element_type=F32) * (FOX_HEAD_DIM ** -0.5)
    k = jnp.dot(h, wqkv_ref[:, d:2 * d], preferred_element_type=F32)
    v = jnp.dot(h, wqkv_ref[:, 2 * d:], preferred_element_type=F32)
    for p in range(n_pairs):
        sl = slice(p * V7X_LANES, (p + 1) * V7X_LANES)
        q_ref[0, p] = q[:, sl].astype(BF16)
        k_ref[0, p] = k[:, sl].astype(BF16)
        v_ref[0, p] = v[:, sl].astype(BF16)

    f_logit = jnp.dot(h, wf_ref[...], preferred_element_type=F32) + bf_ref[...]
    log_f = jnp.minimum(f_logit, 0.0) - jnp.log1p(jnp.exp(-jnp.abs(f_logit)))
    lanes = lax.broadcasted_iota(jnp.int32, log_f.shape, 1)
    log_f = jnp.where(lanes < n_heads, log_f, 0.0)
    tri = tri_ref[...]
    cum = carry_ref[V7X_SUBLANES - 1:]
    for part in _split3(log_f):
        cum = cum + jnp.dot(tri, part, preferred_element_type=F32)
    carry_ref[...] = cum[cum.shape[0] - V7X_SUBLANES:]

    parts = jnp.concatenate(_split3(cum), axis=1)
    augq_ref[...] = (jnp.dot(parts, pq_ref[...], preferred_element_type=F32) + oneq_ref[...]).astype(BF16)
    augk_ref[...] = (jnp.dot(parts, pk_ref[...], preferred_element_type=F32) + onek_ref[...]).astype(BF16)


def _aug_constants(n_heads):
    assert n_heads * AUG_PER_HEAD <= V7X_LANES
    pq = np.zeros((3 * V7X_LANES, V7X_LANES), np.float32)
    pk = np.zeros((3 * V7X_LANES, V7X_LANES), np.float32)
    oneq = np.zeros((1, V7X_LANES), np.float32)
    onek = np.zeros((1, V7X_LANES), np.float32)
    for h in range(n_heads):
        for j in range(3):
            pq[j * V7X_LANES + h, AUG_PER_HEAD * h + j] = 1.0
            pk[j * V7X_LANES + h, AUG_PER_HEAD * h + 3 + j] = -1.0
            oneq[0, AUG_PER_HEAD * h + 3 + j] = 1.0
            onek[0, AUG_PER_HEAD * h + j] = 1.0
    return (jnp.asarray(pq, BF16), jnp.asarray(pk, BF16), jnp.asarray(oneq), jnp.asarray(onek))


def _fox_proj(x2, bsz, seq, shift, scale, g_pre, w_in, b_f):
    t, d = x2.shape
    n_heads = d // FOX_HEAD_DIM
    n_pairs = d // V7X_LANES
    tm = min(ROW_TILE, seq)
    tiles_per_batch = seq // tm
    row, mod, vec = _row_specs(tm, d, tiles_per_batch)
    w_qkv = w_in[:, :3 * d].astype(BF16)
    w_f = jnp.pad(w_in[:, 3 * d:], ((0, 0), (0, V7X_LANES - n_heads))).astype(BF16)
    b_f = jnp.pad(b_f, (0, V7X_LANES - n_heads)).reshape(1, V7X_LANES)
    tri = jnp.asarray(np.tril(np.ones((tm, tm), np.float32)), BF16)
    pq, pk, oneq, onek = _aug_constants(n_heads)
    pair_spec = pl.BlockSpec((1, n_pairs, tm, V7X_LANES),
                             lambda i: (i // tiles_per_batch, 0, i % tiles_per_batch, 0))
    aug_spec = pl.BlockSpec((tm, V7X_LANES), lambda i: (i, 0))
    pair_shape = jax.ShapeDtypeStruct((bsz, n_pairs, seq, V7X_LANES), BF16)
    aug_shape = jax.ShapeDtypeStruct((t, V7X_LANES), BF16)
    return pl.pallas_call(
        functools.partial(_fox_proj_kernel, tiles_per_batch, n_heads),
        grid=(t // tm,),
        in_specs=[row, mod, mod, vec, _resident(w_qkv.shape), _resident(w_f.shape), _resident(b_f.shape),
                  _resident(tri.shape), _resident(pq.shape), _resident(pk.shape), _resident(oneq.shape),
                  _resident(onek.shape)],
        out_specs=[pair_spec, pair_spec, pair_spec, aug_spec, aug_spec],
        out_shape=[pair_shape, pair_shape, pair_shape, aug_shape, aug_shape],
        scratch_shapes=[pltpu.VMEM((V7X_SUBLANES, V7X_LANES), F32)],
        compiler_params=_params(1),
        name="fox_proj",
    )(x2, shift, scale, g_pre, w_qkv, w_f, b_f, tri, pq, pk, oneq, onek)


def _fox_attn_kernel(x_ref, gate_ref, gpost_ref, q_ref, k_ref, v_ref, augq_ref, augk_ref, wout_ref,
                     o_ref, m_ref, acc_ref, heads_ref):
    tq = x_ref.shape[0]
    n_pairs = q_ref.shape[1]
    qi = pl.program_id(1)
    half = lax.broadcasted_iota(jnp.int32, (tq, V7X_LANES), 1) // FOX_HEAD_DIM
    lanes = lax.broadcasted_iota(jnp.int32, (tq, V7X_LANES), 1)
    causal = (lax.broadcasted_iota(jnp.int32, (tq, tq), 1)
              <= lax.broadcasted_iota(jnp.int32, (tq, tq), 0))
    augq = augq_ref[...]
    contract_last = (((1,), (1,)), ((), ()))

    def head(p, e):
        h = 2 * p + e
        own = half == e
        q_own = jnp.where(own, q_ref[0, p], jnp.zeros((), BF16))
        aug_own = jnp.where((lanes >= AUG_PER_HEAD * h) & (lanes < AUG_PER_HEAD * (h + 1)), augq,
                            jnp.zeros((), BF16))
        lhs = jnp.concatenate([q_own, aug_own], axis=1)
        m_ref[...] = jnp.full(m_ref.shape, -jnp.inf, F32)
        acc_ref[...] = jnp.zeros(acc_ref.shape, F32)

        def block(kb, masked):
            rows = pl.ds(pl.multiple_of(kb * tq, tq), tq)
            rhs = jnp.concatenate([k_ref[0, p, rows, :], augk_ref[rows, :]], axis=1)
            s = lax.dot_general(lhs, rhs, contract_last, preferred_element_type=F32)
            if masked:
                s = jnp.where(causal, s, -jnp.inf)
            m_old = m_ref[...]
            m_new = jnp.maximum(m_old, jnp.max(s, axis=1, keepdims=True))
            prob = jnp.exp(s - m_new).astype(BF16)
            v_own = jnp.where(own, v_ref[0, p, rows, :], jnp.ones((), BF16))
            acc_ref[...] = (jnp.exp(m_old - m_new) * acc_ref[...]
                            + jnp.dot(prob, v_own, preferred_element_type=F32))
            m_ref[...] = m_new

        block(qi, True)

        def body(j, carry):
            block(qi - 1 - j, False)
            return carry

        lax.fori_loop(0, qi, body, 0)
        acc = acc_ref[...]
        return acc / pltpu.roll(acc, FOX_HEAD_DIM, axis=1)

    def pair(p, carry):
        o_even = head(p, 0)
        o_odd = head(p, 1)
        heads_ref[p] = jnp.where(half == 0, o_even, o_odd).astype(BF16)
        return carry

    lax.fori_loop(0, n_pairs, pair, 0)
    o = jnp.concatenate([heads_ref[p] for p in range(n_pairs)], axis=1)
    y = jnp.dot(o, wout_ref[...], preferred_element_type=F32)
    o_ref[...] = x_ref[...] + gate_ref[0] * (_rms(y) * gpost_ref[...])


def _fox_attn(x2, bsz, seq, gate, g_post, q, k, v, augq, augk, w_out):
    t, d = x2.shape
    n_pairs = d // V7X_LANES
    tq = min(ATTN_TILE, seq)
    nq = seq // tq
    row = pl.BlockSpec((tq, d), lambda b, i: (b * nq + i, 0))
    mod = pl.BlockSpec((1, 1, d), lambda b, i: (b, 0, 0))
    vec = pl.BlockSpec((1, d), lambda b, i: (0, 0))
    q_spec = pl.BlockSpec((1, n_pairs, tq, V7X_LANES), lambda b, i: (b, 0, i, 0))
    kv_spec = pl.BlockSpec((1, n_pairs, seq, V7X_LANES), lambda b, i: (b, 0, 0, 0))
    augq_spec = pl.BlockSpec((tq, V7X_LANES), lambda b, i: (b * nq + i, 0))
    augk_spec = pl.BlockSpec((seq, V7X_LANES), lambda b, i: (b, 0))
    return pl.pallas_call(
        _fox_attn_kernel,
        grid=(bsz, nq),
        in_specs=[row, mod, vec, q_spec, kv_spec, kv_spec, augq_spec, augk_spec, _resident(w_out.shape)],
        out_specs=row,
        out_shape=jax.ShapeDtypeStruct((t, d), F32),
        scratch_shapes=[pltpu.VMEM((tq, 1), F32), pltpu.VMEM((tq, V7X_LANES), F32),
                        pltpu.VMEM((n_pairs, tq, V7X_LANES), BF16)],
        compiler_params=_params(2),
        name="fox_attn",
    )(x2, gate, g_post, q, k, v, augq, augk, w_out.astype(BF16))


def kernel(x, c, w_cond, b_cond, norm_pre, norm_post, w_ffn_in, w_ffn_out, fox_w_in, fox_b_f, fox_w_out, sconv_w_in, sconv_conv_w, sconv_w_out, lru_w_in, lru_conv_w, lru_conv_b, lru_w_a, lru_b_a, lru_w_x, lru_b_x, lru_lambda, lru_w_out):
    bsz, seq, d = x.shape
    depth = w_cond.shape[0]
    assert d % V7X_LANES == 0 and V7X_LANES == 2 * FOX_HEAD_DIM
    assert seq % min(ROW_TILE, seq) == 0 and seq % min(ATTN_TILE, seq) == 0

    mod = _modulation(c, w_cond, b_cond).reshape(depth, bsz, N_SUB, 3, 1, d)
    x2 = x.reshape(bsz * seq, d)

    for i in range(depth):
        def mods(s):
            return mod[i, :, s, 0], mod[i, :, s, 1], mod[i, :, s, 2]

        def gains(s):
            return norm_pre[i, s].reshape(1, d), norm_post[i, s].reshape(1, d)

        shift, scale, gate = mods(0)
        x2 = _ffn(x2, seq, shift, scale, gate, *gains(0), *_ffn_weights(w_ffn_in[i, 0], w_ffn_out[i, 0]))

        shift, scale, gate = mods(1)
        g_pre, g_post = gains(1)
        kind, j = i % N_MIXERS, i // N_MIXERS
        if kind == 0:
            q, k, v, augq, augk = _fox_proj(x2, bsz, seq, shift, scale, g_pre, fox_w_in[j], fox_b_f[j])
            x2 = _fox_attn(x2, bsz, seq, gate, g_post, q, k, v, augq, augk, fox_w_out[j])
        elif kind == 1:
            x2 = _sconv(x2, seq, shift, scale, gate, g_pre, g_post, sconv_w_in[j], sconv_conv_w[j],
                        sconv_w_out[j])
        else:
            x2 = _lru(x2, seq, shift, scale, gate, g_pre, g_post, lru_w_in[j], lru_conv_w[j],
                      lru_conv_b[j], lru_w_a[j], lru_b_a[j], lru_w_x[j], lru_b_x[j], lru_lambda[j],
                      lru_w_out[j])

        shift, scale, gate = mods(2)
        x2 = _ffn(x2, seq, shift, scale, gate, *gains(2), *_ffn_weights(w_ffn_in[i, 1], w_ffn_out[i, 1]))

    return x2.reshape(bsz, seq, d)
```

```python
import functools

import numpy as np
import jax
import jax.numpy as jnp
from jax import lax
from jax.experimental import pallas as pl
from jax.experimental.pallas import tpu as pltpu

N_SUB = 3
N_MIXERS = 3
FFN_RES_WEIGHT = 0.5
RMS_EPS = 1e-6
FOX_HEAD_DIM = 64
LRU_C = 8.0
LOG2E = 1.4426950408889634

V7X_LANES = 128
V7X_SUBLANES = 8
V7X_MXU_DIM = 256
V7X_VMEM_BYTES = 64 * 1024 * 1024
VMEM_LIMIT_BYTES = V7X_VMEM_BYTES - 8 * 1024 * 1024

ROW_TILE = 512
ATTN_TILE = 256
FFN_CHUNK = V7X_MXU_DIM
AUG_PER_HEAD = 6
ATTN_GROUP_PAIRS = 4
MOD_COL_TILE = 1536

BF16 = jnp.bfloat16
F32 = jnp.float32


def _params(n_axes):
    return pltpu.CompilerParams(dimension_semantics=("arbitrary",) * n_axes,
                                vmem_limit_bytes=VMEM_LIMIT_BYTES)


def _resident(shape):
    return pl.BlockSpec(shape, lambda *_: (0,) * len(shape), pipeline_mode=pl.Buffered(1))


def _rms(x):
    return x * lax.rsqrt(jnp.mean(x * x, axis=-1, keepdims=True) + RMS_EPS)


def _pre(x, gpre_ref, scale_ref, shift_ref):
    return (_rms(x) * gpre_ref[...]) * (1.0 + scale_ref[0]) + shift_ref[0]


def _split3(x):
    hi = x.astype(BF16)
    r1 = x - hi.astype(F32)
    mid = r1.astype(BF16)
    lo = (r1 - mid.astype(F32)).astype(BF16)
    return hi, mid, lo


def _mod_kernel(c_ref, w_ref, b_ref, o_ref):
    c = c_ref[...]
    c_act = (c * jax.nn.sigmoid(c)).astype(BF16)
    o_ref[0] = jnp.dot(c_act, w_ref[0].astype(BF16), preferred_element_type=F32) + b_ref[0]


def _modulation(c, w_cond, b_cond):
    depth, d, n = w_cond.shape
    bsz = c.shape[0]
    tn = MOD_COL_TILE if n % MOD_COL_TILE == 0 else n
    return pl.pallas_call(
        _mod_kernel,
        grid=(depth, n // tn),
        in_specs=[
            pl.BlockSpec((bsz, d), lambda i, j: (0, 0)),
            pl.BlockSpec((1, d, tn), lambda i, j: (i, 0, j)),
            pl.BlockSpec((1, 1, tn), lambda i, j: (i, 0, j)),
        ],
        out_specs=pl.BlockSpec((1, bsz, tn), lambda i, j: (i, 0, j)),
        out_shape=jax.ShapeDtypeStruct((depth, bsz, n), F32),
        compiler_params=_params(2),
        name="adaln_mod",
    )(c, w_cond, b_cond.reshape(depth, 1, n))


def _ffn_kernel(x_ref, shift_ref, scale_ref, gate_ref, gpre_ref, gpost_ref, win_ref, wout_ref,
                o_ref, act_ref):
    x = x_ref[...]
    h = _pre(x, gpre_ref, scale_ref, shift_ref).astype(BF16)
    f = wout_ref.shape[0]
    fc = FFN_CHUNK
    for c in range(f // fc):
        g = jnp.dot(h, win_ref[:, c * fc:(c + 1) * fc], preferred_element_type=F32)
        u = jnp.dot(h, win_ref[:, f + c * fc:f + (c + 1) * fc], preferred_element_type=F32)
        act_ref[:, c * fc:(c + 1) * fc] = (g * jax.nn.sigmoid(g) * u).astype(BF16)
    y = jnp.dot(act_ref[...], wout_ref[...], preferred_element_type=F32)
    o_ref[...] = x + (FFN_RES_WEIGHT * gate_ref[0]) * (_rms(y) * gpost_ref[...])


def _row_specs(tm, d, tiles_per_batch):
    row = pl.BlockSpec((tm, d), lambda i: (i, 0))
    mod = pl.BlockSpec((1, 1, d), lambda i: (i // tiles_per_batch, 0, 0))
    vec = pl.BlockSpec((1, d), lambda i: (0, 0))
    return row, mod, vec


def _stacked_resident(stacked, index):
    lead = len(index)
    shape = (None,) * lead + tuple(stacked.shape[lead:])
    return pl.BlockSpec(shape, lambda *_: tuple(index) + (0,) * (stacked.ndim - lead),
                        pipeline_mode=pl.Buffered(1))


def _ffn(x2, seq, shift, scale, gate, g_pre, g_post, w_in, w_out, index):
    t, d = x2.shape
    tm = min(ROW_TILE, seq)
    f = w_out.shape[-2]
    assert f % FFN_CHUNK == 0
    row, mod, vec = _row_specs(tm, d, seq // tm)
    return pl.pallas_call(
        _ffn_kernel,
        grid=(t // tm,),
        in_specs=[row, mod, mod, mod, vec, vec, _stacked_resident(w_in, index),
                  _stacked_resident(w_out, index)],
        out_specs=row,
        out_shape=jax.ShapeDtypeStruct((t, d), F32),
        scratch_shapes=[pltpu.VMEM((tm, f), BF16)],
        compiler_params=_params(1),
        name="ffn",
    )(x2, shift, scale, gate, g_pre, g_post, w_in, w_out)


def _shift_rows(u, prev, k):
    rows = lax.broadcasted_iota(jnp.int32, u.shape, 0)
    out = pltpu.roll(u, k, axis=0)
    for j in range(k):
        out = jnp.where(rows == j, prev[V7X_SUBLANES - k + j:V7X_SUBLANES - k + j + 1], out)
    return out


def _causal_conv(u, prev, w_ref):
    kw = w_ref.shape[0]
    out = w_ref[kw - 1:kw] * u
    for k in range(1, kw):
        out = out + w_ref[kw - 1 - k:kw - k] * _shift_rows(u, prev, k)
    return out


def _sconv_kernel(tiles_per_batch, x_ref, shift_ref, scale_ref, gate_ref, gpre_ref, gpost_ref,
                  win_ref, convw_ref, wout_ref, o_ref, carry_ref):
    d = x_ref.shape[1]
    tm = x_ref.shape[0]

    @pl.when(pl.program_id(0) % tiles_per_batch == 0)
    def _():
        carry_ref[...] = jnp.zeros(carry_ref.shape, F32)

    x = x_ref[...]
    h = _pre(x, gpre_ref, scale_ref, shift_ref).astype(BF16)
    c_gate = jnp.dot(h, win_ref[:, d:2 * d], preferred_element_type=F32)
    xv = jnp.dot(h, win_ref[:, 2 * d:], preferred_element_type=F32)
    u = c_gate * xv
    conv = _causal_conv(u, carry_ref[...], convw_ref)
    carry_ref[...] = u[tm - V7X_SUBLANES:]
    b_gate = jnp.dot(h, win_ref[:, :d], preferred_element_type=F32)
    y = jnp.dot((b_gate * conv).astype(BF16), wout_ref[...], preferred_element_type=F32)
    o_ref[...] = x + gate_ref[0] * (_rms(y) * gpost_ref[...])


def _sconv(x2, seq, shift, scale, gate, g_pre, g_post, w_in, conv_w, w_out):
    t, d = x2.shape
    tm = min(ROW_TILE, seq)
    row, mod, vec = _row_specs(tm, d, seq // tm)
    return pl.pallas_call(
        functools.partial(_sconv_kernel, seq // tm),
        grid=(t // tm,),
        in_specs=[row, mod, mod, mod, vec, vec, _resident(w_in.shape), _resident(conv_w.shape),
                  _resident(w_out.shape)],
        out_specs=row,
        out_shape=jax.ShapeDtypeStruct((t, d), F32),
        scratch_shapes=[pltpu.VMEM((V7X_SUBLANES, d), F32)],
        compiler_params=_params(1),
        name="sconv_mixer",
    )(x2, shift, scale, gate, g_pre, g_post, w_in.astype(BF16), conv_w, w_out.astype(BF16))


def _linear_scan(a, b):
    n = a.shape[0]
    rows = lax.broadcasted_iota(jnp.int32, a.shape, 0)
    d = 1
    while d < n:
        keep = rows >= d
        b = jnp.where(keep, b + a * pltpu.roll(b, d, axis=0), b)
        a = jnp.where(keep, a * pltpu.roll(a, d, axis=0), a)
        d *= 2
    return a, b


def _lru_kernel(tiles_per_batch, x_ref, shift_ref, scale_ref, gate_ref, gpre_ref, gpost_ref,
                win_ref, convw_ref, convb_ref, wgate_ref, ba_ref, bx_ref, lam_ref, wout_ref,
                o_ref, xcarry_ref, hcarry_ref):
    tm = x_ref.shape[0]
    w = wout_ref.shape[0]
    n_groups, gk, two_gn = wgate_ref.shape
    gn = two_gn // 2

    @pl.when(pl.program_id(0) % tiles_per_batch == 0)
    def _():
        xcarry_ref[...] = jnp.zeros(xcarry_ref.shape, F32)
        hcarry_ref[...] = jnp.zeros(hcarry_ref.shape, F32)

    x = x_ref[...]
    h = _pre(x, gpre_ref, scale_ref, shift_ref).astype(BF16)
    xraw = jnp.dot(h, win_ref[:, w:], preferred_element_type=F32)
    xb = _causal_conv(xraw, xcarry_ref[...], convw_ref) + convb_ref[...]
    xcarry_ref[...] = xraw[tm - V7X_SUBLANES:]

    xb16 = xb.astype(BF16)
    r_parts, i_parts = [], []
    for g in range(n_groups):
        ri = jnp.dot(xb16[:, g * gk:(g + 1) * gk], wgate_ref[g], preferred_element_type=F32)
        r_parts.append(ri[:, :gn])
        i_parts.append(ri[:, gn:])
    r = jax.nn.sigmoid(jnp.concatenate(r_parts, axis=1) + ba_ref[...])
    i = jax.nn.sigmoid(jnp.concatenate(i_parts, axis=1) + bx_ref[...])

    neg_lam = -lam_ref[...]
    softplus = jnp.maximum(neg_lam, 0.0) + jnp.log1p(jnp.exp(-jnp.abs(neg_lam)))
    log_a = (-LRU_C * softplus) * r
    a = jnp.exp(log_a)
    mult = jnp.sqrt(-jnp.tanh(log_a) * (1.0 + a * a))
    a_cum, b_cum = _linear_scan(a, mult * (i * xb))
    hs = a_cum * hcarry_ref[V7X_SUBLANES - 1:] + b_cum
    hcarry_ref[...] = hs[tm - V7X_SUBLANES:]

    gate_br = jnp.dot(h, win_ref[:, :w], preferred_element_type=F32)
    y = jnp.dot((hs * jax.nn.gelu(gate_br)).astype(BF16), wout_ref[...], preferred_element_type=F32)
    o_ref[...] = x + gate_ref[0] * (_rms(y) * gpost_ref[...])


def _lru_gate_weights(w_a, w_x):
    nb, bd, _ = w_a.shape
    per = V7X_MXU_DIM // bd
    assert nb % per == 0

    def dense(wb):
        wg = wb.reshape(nb // per, per, bd, bd)
        eye = jnp.eye(per, dtype=wb.dtype)
        return jnp.einsum("gpij,pq->gpiqj", wg, eye).reshape(nb // per, per * bd, per * bd)

    return jnp.concatenate([dense(w_a), dense(w_x)], axis=-1).astype(BF16)


def _lru(x2, seq, shift, scale, gate, g_pre, g_post, w_in, conv_w, conv_b, w_a, b_a, w_x, b_x, lam,
         w_out):
    t, d = x2.shape
    w = w_out.shape[0]
    tm = min(ROW_TILE, seq)
    row, mod, vec = _row_specs(tm, d, seq // tm)
    wvec = pl.BlockSpec((1, w), lambda i: (0, 0))
    w_gate = _lru_gate_weights(w_a, w_x)
    return pl.pallas_call(
        functools.partial(_lru_kernel, seq // tm),
        grid=(t // tm,),
        in_specs=[row, mod, mod, mod, vec, vec, _resident(w_in.shape), _resident(conv_w.shape), wvec,
                  _resident(w_gate.shape), wvec, wvec, wvec, _resident(w_out.shape)],
        out_specs=row,
        out_shape=jax.ShapeDtypeStruct((t, d), F32),
        scratch_shapes=[pltpu.VMEM((V7X_SUBLANES, w), F32), pltpu.VMEM((V7X_SUBLANES, w), F32)],
        compiler_params=_params(1),
        name="lru_mixer",
    )(x2, shift, scale, gate, g_pre, g_post, w_in.astype(BF16), conv_w, conv_b.reshape(1, w), w_gate,
      b_a.reshape(1, w), b_x.reshape(1, w), lam.reshape(1, w), w_out.astype(BF16))


def _fox_proj_kernel(tiles_per_batch, n_heads, x_ref, shift_ref, scale_ref, gpre_ref, wqkv_ref, wf_ref, bf_ref,
                     tri_ref, pq_ref, pk_ref, oneq_ref, onek_ref,
                     q_ref, k_ref, v_ref, augq_ref, augk_ref, carry_ref):
    d = x_ref.shape[1]
    n_pairs = q_ref.shape[1]

    @pl.when(pl.program_id(0) % tiles_per_batch == 0)
    def _():
        carry_ref[...] = jnp.zeros(carry_ref.shape, F32)

    h = _pre(x_ref[...], gpre_ref, scale_ref, shift_ref).astype(BF16)
    q = jnp.dot(h, wqkv_ref[:, :d], preferred_element_type=F32) * (FOX_HEAD_DIM ** -0.5 * LOG2E)
    k = jnp.dot(h, wqkv_ref[:, d:2 * d], preferred_element_type=F32)
    v = jnp.dot(h, wqkv_ref[:, 2 * d:], preferred_element_type=F32)
    for p in range(n_pairs):
        sl = slice(p * V7X_LANES, (p + 1) * V7X_LANES)
        q_ref[0, p] = q[:, sl].astype(BF16)
        k_ref[0, p] = k[:, sl].astype(BF16)
        v_ref[0, p] = v[:, sl].astype(BF16)

    f_logit = jnp.dot(h, wf_ref[...], preferred_element_type=F32) + bf_ref[...]
    log_f = jnp.minimum(f_logit, 0.0) - jnp.log1p(jnp.exp(-jnp.abs(f_logit)))
    lanes = lax.broadcasted_iota(jnp.int32, log_f.shape, 1)
    log_f = jnp.where(lanes < n_heads, log_f, 0.0)
    tri = tri_ref[...]
    cum = carry_ref[V7X_SUBLANES - 1:]
    for part in _split3(log_f):
        cum = cum + jnp.dot(tri, part, preferred_element_type=F32)
    carry_ref[...] = cum[cum.shape[0] - V7X_SUBLANES:]

    parts = jnp.concatenate(_split3(cum * LOG2E), axis=1)
    augq_ref[...] = (jnp.dot(parts, pq_ref[...], preferred_element_type=F32) + oneq_ref[...]).astype(BF16)
    augk_ref[...] = (jnp.dot(parts, pk_ref[...], preferred_element_type=F32) + onek_ref[...]).astype(BF16)


def _aug_constants(n_heads):
    assert n_heads * AUG_PER_HEAD <= V7X_LANES
    pq = np.zeros((3 * V7X_LANES, V7X_LANES), np.float32)
    pk = np.zeros((3 * V7X_LANES, V7X_LANES), np.float32)
    oneq = np.zeros((1, V7X_LANES), np.float32)
    onek = np.zeros((1, V7X_LANES), np.float32)
    for h in range(n_heads):
        for j in range(3):
            pq[j * V7X_LANES + h, AUG_PER_HEAD * h + j] = 1.0
            pk[j * V7X_LANES + h, AUG_PER_HEAD * h + 3 + j] = -1.0
            oneq[0, AUG_PER_HEAD * h + 3 + j] = 1.0
            onek[0, AUG_PER_HEAD * h + j] = 1.0
    return (jnp.asarray(pq, BF16), jnp.asarray(pk, BF16), jnp.asarray(oneq), jnp.asarray(onek))


def _fox_proj(x2, bsz, seq, shift, scale, g_pre, w_in, b_f):
    t, d = x2.shape
    n_heads = d // FOX_HEAD_DIM
    n_pairs = d // V7X_LANES
    tm = min(ROW_TILE, seq)
    tiles_per_batch = seq // tm
    row, mod, vec = _row_specs(tm, d, tiles_per_batch)
    w_qkv = w_in[:, :3 * d].astype(BF16)
    w_f = jnp.pad(w_in[:, 3 * d:], ((0, 0), (0, V7X_LANES - n_heads))).astype(BF16)
    b_f = jnp.pad(b_f, (0, V7X_LANES - n_heads)).reshape(1, V7X_LANES)
    tri = jnp.asarray(np.tril(np.ones((tm, tm), np.float32)), BF16)
    pq, pk, oneq, onek = _aug_constants(n_heads)
    pair_spec = pl.BlockSpec((1, n_pairs, tm, V7X_LANES),
                             lambda i: (i // tiles_per_batch, 0, i % tiles_per_batch, 0))
    aug_spec = pl.BlockSpec((tm, V7X_LANES), lambda i: (i, 0))
    pair_shape = jax.ShapeDtypeStruct((bsz, n_pairs, seq, V7X_LANES), BF16)
    aug_shape = jax.ShapeDtypeStruct((t, V7X_LANES), BF16)
    return pl.pallas_call(
        functools.partial(_fox_proj_kernel, tiles_per_batch, n_heads),
        grid=(t // tm,),
        in_specs=[row, mod, mod, vec, _resident(w_qkv.shape), _resident(w_f.shape), _resident(b_f.shape),
                  _resident(tri.shape), _resident(pq.shape), _resident(pk.shape), _resident(oneq.shape),
                  _resident(onek.shape)],
        out_specs=[pair_spec, pair_spec, pair_spec, aug_spec, aug_spec],
        out_shape=[pair_shape, pair_shape, pair_shape, aug_shape, aug_shape],
        scratch_shapes=[pltpu.VMEM((V7X_SUBLANES, V7X_LANES), F32)],
        compiler_params=_params(1),
        name="fox_proj",
    )(x2, shift, scale, g_pre, w_qkv, w_f, b_f, tri, pq, pk, oneq, onek)


def _fox_attn_kernel(x_ref, gate_ref, gpost_ref, q_ref, k_ref, v_ref, augq_ref, augk_ref, wout_ref,
                     o_ref, lhs_ref, s_ref, mx_ref, acc_ref, heads_ref):
    tq = x_ref.shape[0]
    n_pairs = q_ref.shape[1]
    group_heads = lhs_ref.shape[0]
    group_pairs = group_heads // 2
    qi = pl.program_id(1)
    lanes = lax.broadcasted_iota(jnp.int32, (tq, V7X_LANES), 1)
    half = lanes // FOX_HEAD_DIM
    causal = (lax.broadcasted_iota(jnp.int32, (tq, tq), 1)
              <= lax.broadcasted_iota(jnp.int32, (tq, tq), 0))
    augq = augq_ref[...]
    contract_last = (((1,), (1,)), ((), ()))
    zero16 = jnp.zeros((), BF16)

    def key_rows(kb):
        return pl.ds(pl.multiple_of(kb * tq, tq), tq)

    def group(g, carry):
        for pp in range(group_pairs):
            p = g * group_pairs + pp
            q_pair = q_ref[0, p]
            for e in range(2):
                h = 2 * p + e
                own_aug = (lanes >= AUG_PER_HEAD * h) & (lanes < AUG_PER_HEAD * (h + 1))
                lhs_ref[2 * pp + e] = jnp.concatenate(
                    [jnp.where(half == e, q_pair, zero16), jnp.where(own_aug, augq, zero16)], axis=1)
        mx_ref[...] = jnp.full(mx_ref.shape, -jnp.inf, F32)
        acc_ref[...] = jnp.zeros(acc_ref.shape, F32)

        def scores(kb, masked):
            rows = key_rows(kb)
            aug_k = augk_ref[rows, :]
            for pp in range(group_pairs):
                rhs = jnp.concatenate([k_ref[0, g * group_pairs + pp, rows, :], aug_k], axis=1)
                for e in range(2):
                    hh = 2 * pp + e
                    s = lax.dot_general(lhs_ref[hh], rhs, contract_last, preferred_element_type=F32)
                    if masked:
                        s = jnp.where(causal, s, -jnp.inf)
                    s_ref[hh, kb] = s
                    mx_ref[hh] = jnp.maximum(mx_ref[hh],
                                             jnp.maximum(s[:, :V7X_LANES], s[:, V7X_LANES:]))

        scores(qi, True)

        def scores_body(kb, c):
            scores(kb, False)
            return c

        lax.fori_loop(0, qi, scores_body, 0)

        for hh in range(group_heads):
            mx_ref[hh] = jnp.broadcast_to(jnp.max(mx_ref[hh], axis=1, keepdims=True), (tq, V7X_LANES))

        def values_body(kb, c):
            rows = key_rows(kb)
            for pp in range(group_pairs):
                v_pair = v_ref[0, g * group_pairs + pp, rows, :]
                for e in range(2):
                    hh = 2 * pp + e
                    m = mx_ref[hh]
                    s = s_ref[hh, kb]
                    prob = jnp.concatenate([jnp.exp2(s[:, :V7X_LANES] - m), jnp.exp2(s[:, V7X_LANES:] - m)],
                                           axis=1).astype(BF16)
                    v_own = jnp.where(half == e, v_pair, jnp.ones((), BF16))
                    acc_ref[hh] += jnp.dot(prob, v_own, preferred_element_type=F32)
            return c

        lax.fori_loop(0, qi + 1, values_body, 0)

        for pp in range(group_pairs):
            a0 = acc_ref[2 * pp]
            a1 = acc_ref[2 * pp + 1]
            o_even = a0 / pltpu.roll(a0, FOX_HEAD_DIM, axis=1)
            o_odd = a1 / pltpu.roll(a1, FOX_HEAD_DIM, axis=1)
            heads_ref[g * group_pairs + pp] = jnp.where(half == 0, o_even, o_odd).astype(BF16)
        return carry

    lax.fori_loop(0, n_pairs // group_pairs, group, 0)
    o = jnp.concatenate([heads_ref[p] for p in range(n_pairs)], axis=1)
    y = jnp.dot(o, wout_ref[...], preferred_element_type=F32)
    o_ref[...] = x_ref[...] + gate_ref[0] * (_rms(y) * gpost_ref[...])


def _fox_attn(x2, bsz, seq, gate, g_post, q, k, v, augq, augk, w_out):
    t, d = x2.shape
    n_pairs = d // V7X_LANES
    tq = min(ATTN_TILE, seq)
    nq = seq // tq
    group_pairs = min(ATTN_GROUP_PAIRS, n_pairs)
    assert n_pairs % group_pairs == 0
    row = pl.BlockSpec((tq, d), lambda b, i: (b * nq + i, 0))
    mod = pl.BlockSpec((1, 1, d), lambda b, i: (b, 0, 0))
    vec = pl.BlockSpec((1, d), lambda b, i: (0, 0))
    q_spec = pl.BlockSpec((1, n_pairs, tq, V7X_LANES), lambda b, i: (b, 0, i, 0))
    kv_spec = pl.BlockSpec((1, n_pairs, seq, V7X_LANES), lambda b, i: (b, 0, 0, 0))
    augq_spec = pl.BlockSpec((tq, V7X_LANES), lambda b, i: (b * nq + i, 0))
    augk_spec = pl.BlockSpec((seq, V7X_LANES), lambda b, i: (b, 0))
    return pl.pallas_call(
        _fox_attn_kernel,
        grid=(bsz, nq),
        in_specs=[row, mod, vec, q_spec, kv_spec, kv_spec, augq_spec, augk_spec, _resident(w_out.shape)],
        out_specs=row,
        out_shape=jax.ShapeDtypeStruct((t, d), F32),
        scratch_shapes=[pltpu.VMEM((2 * group_pairs, tq, 2 * V7X_LANES), BF16),
                        pltpu.VMEM((2 * group_pairs, nq, tq, tq), F32),
                        pltpu.VMEM((2 * group_pairs, tq, V7X_LANES), F32),
                        pltpu.VMEM((2 * group_pairs, tq, V7X_LANES), F32),
                        pltpu.VMEM((n_pairs, tq, V7X_LANES), BF16)],
        compiler_params=_params(2),
        name="fox_attn",
    )(x2, gate, g_post, q, k, v, augq, augk, w_out.astype(BF16))


def kernel(x, c, w_cond, b_cond, norm_pre, norm_post, w_ffn_in, w_ffn_out, fox_w_in, fox_b_f, fox_w_out, sconv_w_in, sconv_conv_w, sconv_w_out, lru_w_in, lru_conv_w, lru_conv_b, lru_w_a, lru_b_a, lru_w_x, lru_b_x, lru_lambda, lru_w_out):
    bsz, seq, d = x.shape
    depth = w_cond.shape[0]
    assert d % V7X_LANES == 0 and V7X_LANES == 2 * FOX_HEAD_DIM
    assert seq % min(ROW_TILE, seq) == 0 and seq % min(ATTN_TILE, seq) == 0

    mod = _modulation(c, w_cond, b_cond).reshape(depth, bsz, N_SUB, 3, 1, d)
    x2 = x.reshape(bsz * seq, d)
    w_ffn_in16 = w_ffn_in.astype(BF16)
    w_ffn_out16 = w_ffn_out.astype(BF16)

    for i in range(depth):
        def mods(s):
            return mod[i, :, s, 0], mod[i, :, s, 1], mod[i, :, s, 2]

        def gains(s):
            return norm_pre[i, s].reshape(1, d), norm_post[i, s].reshape(1, d)

        shift, scale, gate = mods(0)
        x2 = _ffn(x2, seq, shift, scale, gate, *gains(0), w_ffn_in16, w_ffn_out16, (i, 0))

        shift, scale, gate = mods(1)
        g_pre, g_post = gains(1)
        kind, j = i % N_MIXERS, i // N_MIXERS
        if kind == 0:
            q, k, v, augq, augk = _fox_proj(x2, bsz, seq, shift, scale, g_pre, fox_w_in[j], fox_b_f[j])
            x2 = _fox_attn(x2, bsz, seq, gate, g_post, q, k, v, augq, augk, fox_w_out[j])
        elif kind == 1:
            x2 = _sconv(x2, seq, shift, scale, gate, g_pre, g_post, sconv_w_in[j], sconv_conv_w[j],
                        sconv_w_out[j])
        else:
            x2 = _lru(x2, seq, shift, scale, gate, g_pre, g_post, lru_w_in[j], lru_conv_w[j],
                      lru_conv_b[j], lru_w_a[j], lru_b_a[j], lru_w_x[j], lru_b_x[j], lru_lambda[j],
                      lru_w_out[j])

        shift, scale, gate = mods(2)
        x2 = _ffn(x2, seq, shift, scale, gate, *gains(2), w_ffn_in16, w_ffn_out16, (i, 1))

    return x2.reshape(bsz, seq, d)
```

```python
import functools

import numpy as np
import jax
import jax.numpy as jnp
from jax import lax
from jax.experimental import pallas as pl
from jax.experimental.pallas import tpu as pltpu

N_SUB = 3
N_MIXERS = 3
FFN_RES_WEIGHT = 0.5
RMS_EPS = 1e-6
FOX_HEAD_DIM = 64
LRU_C = 8.0
LOG2E = 1.4426950408889634

V7X_LANES = 128
V7X_SUBLANES = 8
V7X_MXU_DIM = 256
V7X_VMEM_BYTES = 64 * 1024 * 1024
VMEM_LIMIT_BYTES = V7X_VMEM_BYTES - 8 * 1024 * 1024

ROW_TILE = 512
ATTN_TILE = 256
FFN_CHUNK = V7X_MXU_DIM
FFN_ROW_TILE = 1024
FFN_SUBTILE = 512
AUG_PER_HEAD = 6
ATTN_GROUP_PAIRS = 4
MOD_COL_TILE = 1536

BF16 = jnp.bfloat16
F32 = jnp.float32


def _params(n_axes):
    return pltpu.CompilerParams(dimension_semantics=("arbitrary",) * n_axes,
                                vmem_limit_bytes=VMEM_LIMIT_BYTES)


def _resident(shape):
    return pl.BlockSpec(shape, lambda *_: (0,) * len(shape), pipeline_mode=pl.Buffered(1))


def _rms(x):
    return x * lax.rsqrt(jnp.mean(x * x, axis=-1, keepdims=True) + RMS_EPS)


def _pre(x, gpre_ref, scale_ref, shift_ref):
    return (_rms(x) * gpre_ref[...]) * (1.0 + scale_ref[0]) + shift_ref[0]


def _split3(x):
    hi = x.astype(BF16)
    r1 = x - hi.astype(F32)
    mid = r1.astype(BF16)
    lo = (r1 - mid.astype(F32)).astype(BF16)
    return hi, mid, lo


def _mod_kernel(c_ref, w_ref, b_ref, o_ref):
    c = c_ref[...]
    c_act = (c * jax.nn.sigmoid(c)).astype(BF16)
    o_ref[0] = jnp.dot(c_act, w_ref[0].astype(BF16), preferred_element_type=F32) + b_ref[0]


def _modulation(c, w_cond, b_cond):
    depth, d, n = w_cond.shape
    bsz = c.shape[0]
    tn = MOD_COL_TILE if n % MOD_COL_TILE == 0 else n
    return pl.pallas_call(
        _mod_kernel,
        grid=(depth, n // tn),
        in_specs=[
            pl.BlockSpec((bsz, d), lambda i, j: (0, 0)),
            pl.BlockSpec((1, d, tn), lambda i, j: (i, 0, j)),
            pl.BlockSpec((1, 1, tn), lambda i, j: (i, 0, j)),
        ],
        out_specs=pl.BlockSpec((1, bsz, tn), lambda i, j: (i, 0, j)),
        out_shape=jax.ShapeDtypeStruct((depth, bsz, n), F32),
        compiler_params=_params(2),
        name="adaln_mod",
    )(c, w_cond, b_cond.reshape(depth, 1, n))


def _ffn_kernel(x_ref, shift_ref, scale_ref, gate_ref, gpre_ref, gpost_ref, win_ref, wout_ref,
                o_ref, act_ref):
    f = wout_ref.shape[0]
    fc = FFN_CHUNK
    tm = x_ref.shape[0]
    for r in range(0, tm, FFN_SUBTILE):
        rows = slice(r, r + FFN_SUBTILE)
        x = x_ref[rows, :]
        h = _pre(x, gpre_ref, scale_ref, shift_ref).astype(BF16)
        for c in range(f // fc):
            g = jnp.dot(h, win_ref[:, c * fc:(c + 1) * fc], preferred_element_type=F32)
            u = jnp.dot(h, win_ref[:, f + c * fc:f + (c + 1) * fc], preferred_element_type=F32)
            act_ref[rows, c * fc:(c + 1) * fc] = (g * jax.nn.sigmoid(g) * u).astype(BF16)
        y = jnp.dot(act_ref[rows, :], wout_ref[...], preferred_element_type=F32)
        o_ref[rows, :] = x + (FFN_RES_WEIGHT * gate_ref[0]) * (_rms(y) * gpost_ref[...])


def _row_specs(tm, d, tiles_per_batch):
    row = pl.BlockSpec((tm, d), lambda i: (i, 0))
    mod = pl.BlockSpec((1, 1, d), lambda i: (i // tiles_per_batch, 0, 0))
    vec = pl.BlockSpec((1, d), lambda i: (0, 0))
    return row, mod, vec


def _stacked_resident(stacked, index):
    lead = len(index)
    shape = (None,) * lead + tuple(stacked.shape[lead:])
    return pl.BlockSpec(shape, lambda *_: tuple(index) + (0,) * (stacked.ndim - lead),
                        pipeline_mode=pl.Buffered(1))


def _ffn(x2, seq, shift, scale, gate, g_pre, g_post, w_in, w_out, index):
    t, d = x2.shape
    tm = min(FFN_ROW_TILE, seq)
    f = w_out.shape[-2]
    assert f % FFN_CHUNK == 0 and tm % FFN_SUBTILE == 0
    row, mod, vec = _row_specs(tm, d, seq // tm)
    return pl.pallas_call(
        _ffn_kernel,
        grid=(t // tm,),
        in_specs=[row, mod, mod, mod, vec, vec, _stacked_resident(w_in, index),
                  _stacked_resident(w_out, index)],
        out_specs=row,
        out_shape=jax.ShapeDtypeStruct((t, d), F32),
        scratch_shapes=[pltpu.VMEM((tm, f), BF16)],
        compiler_params=_params(1),
        name="ffn",
    )(x2, shift, scale, gate, g_pre, g_post, w_in, w_out)


def _causal_conv(u, hist_ref, w_ref):
    tm = u.shape[0]
    kw = w_ref.shape[0]
    hist_ref[V7X_SUBLANES:, :] = u
    out = w_ref[kw - 1:kw] * u
    for k in range(1, kw):
        out = out + w_ref[kw - 1 - k:kw - k] * hist_ref[V7X_SUBLANES - k:V7X_SUBLANES - k + tm, :]
    hist_ref[:V7X_SUBLANES, :] = u[tm - V7X_SUBLANES:]
    return out


def _sconv_kernel(tiles_per_batch, x_ref, shift_ref, scale_ref, gate_ref, gpre_ref, gpost_ref,
                  win_ref, convw_ref, wout_ref, o_ref, hist_ref):
    d = x_ref.shape[1]

    @pl.when(pl.program_id(0) % tiles_per_batch == 0)
    def _():
        hist_ref[:V7X_SUBLANES, :] = jnp.zeros((V7X_SUBLANES, d), F32)

    x = x_ref[...]
    h = _pre(x, gpre_ref, scale_ref, shift_ref).astype(BF16)
    c_gate = jnp.dot(h, win_ref[:, d:2 * d], preferred_element_type=F32)
    xv = jnp.dot(h, win_ref[:, 2 * d:], preferred_element_type=F32)
    u = c_gate * xv
    conv = _causal_conv(u, hist_ref, convw_ref)
    b_gate = jnp.dot(h, win_ref[:, :d], preferred_element_type=F32)
    y = jnp.dot((b_gate * conv).astype(BF16), wout_ref[...], preferred_element_type=F32)
    o_ref[...] = x + gate_ref[0] * (_rms(y) * gpost_ref[...])


def _sconv(x2, seq, shift, scale, gate, g_pre, g_post, w_in, conv_w, w_out):
    t, d = x2.shape
    tm = min(ROW_TILE, seq)
    row, mod, vec = _row_specs(tm, d, seq // tm)
    return pl.pallas_call(
        functools.partial(_sconv_kernel, seq // tm),
        grid=(t // tm,),
        in_specs=[row, mod, mod, mod, vec, vec, _resident(w_in.shape), _resident(conv_w.shape),
                  _resident(w_out.shape)],
        out_specs=row,
        out_shape=jax.ShapeDtypeStruct((t, d), F32),
        scratch_shapes=[pltpu.VMEM((V7X_SUBLANES + tm, d), F32)],
        compiler_params=_params(1),
        name="sconv_mixer",
    )(x2, shift, scale, gate, g_pre, g_post, w_in.astype(BF16), conv_w, w_out.astype(BF16))


def _linear_scan(a, b, h_prev):
    n = a.shape[0]
    sub = lax.broadcasted_iota(jnp.int32, a.shape, 0) % V7X_SUBLANES
    d = 1
    while d < V7X_SUBLANES:
        keep = sub >= d
        b = jnp.where(keep, b + a * pltpu.roll(b, d, axis=0), b)
        a = jnp.where(keep, a * pltpu.roll(a, d, axis=0), a)
        d *= 2
    groups = []
    for r in range(0, n, V7X_SUBLANES):
        h_group = a[r:r + V7X_SUBLANES] * h_prev + b[r:r + V7X_SUBLANES]
        groups.append(h_group)
        h_prev = h_group[V7X_SUBLANES - 1:]
    return jnp.concatenate(groups, axis=0)


def _lru_kernel(tiles_per_batch, x_ref, shift_ref, scale_ref, gate_ref, gpre_ref, gpost_ref,
                win_ref, convw_ref, convb_ref, wgate_ref, ba_ref, bx_ref, lam_ref, wout_ref,
                o_ref, xhist_ref, hcarry_ref):
    tm = x_ref.shape[0]
    w = wout_ref.shape[0]
    n_groups, gk, two_gn = wgate_ref.shape
    gn = two_gn // 2

    @pl.when(pl.program_id(0) % tiles_per_batch == 0)
    def _():
        xhist_ref[:V7X_SUBLANES, :] = jnp.zeros((V7X_SUBLANES, w), F32)
        hcarry_ref[...] = jnp.zeros(hcarry_ref.shape, F32)

    x = x_ref[...]
    h = _pre(x, gpre_ref, scale_ref, shift_ref).astype(BF16)
    xraw = jnp.dot(h, win_ref[:, w:], preferred_element_type=F32)
    xb = _causal_conv(xraw, xhist_ref, convw_ref) + convb_ref[...]

    xb16 = xb.astype(BF16)
    r_parts, i_parts = [], []
    for g in range(n_groups):
        ri = jnp.dot(xb16[:, g * gk:(g + 1) * gk], wgate_ref[g], preferred_element_type=F32)
        r_parts.append(ri[:, :gn])
        i_parts.append(ri[:, gn:])
    r = jax.nn.sigmoid(jnp.concatenate(r_parts, axis=1) + ba_ref[...])
    i = jax.nn.sigmoid(jnp.concatenate(i_parts, axis=1) + bx_ref[...])

    neg_lam = -lam_ref[...]
    softplus = jnp.maximum(neg_lam, 0.0) + jnp.log1p(jnp.exp(-jnp.abs(neg_lam)))
    log_a = (-LRU_C * softplus) * r
    a = jnp.exp(log_a)
    mult = jnp.sqrt(-jnp.tanh(log_a) * (1.0 + a * a))
    hs = _linear_scan(a, mult * (i * xb), hcarry_ref[V7X_SUBLANES - 1:])
    hcarry_ref[...] = hs[tm - V7X_SUBLANES:]

    gate_br = jnp.dot(h, win_ref[:, :w], preferred_element_type=F32)
    y = jnp.dot((hs * jax.nn.gelu(gate_br)).astype(BF16), wout_ref[...], preferred_element_type=F32)
    o_ref[...] = x + gate_ref[0] * (_rms(y) * gpost_ref[...])


def _lru_gate_weights(w_a, w_x):
    nb, bd, _ = w_a.shape
    per = V7X_MXU_DIM // bd
    assert nb % per == 0

    def dense(wb):
        wg = wb.reshape(nb // per, per, bd, bd)
        eye = jnp.eye(per, dtype=wb.dtype)
        return jnp.einsum("gpij,pq->gpiqj", wg, eye).reshape(nb // per, per * bd, per * bd)

    return jnp.concatenate([dense(w_a), dense(w_x)], axis=-1).astype(BF16)


def _lru(x2, seq, shift, scale, gate, g_pre, g_post, w_in, conv_w, conv_b, w_a, b_a, w_x, b_x, lam,
         w_out):
    t, d = x2.shape
    w = w_out.shape[0]
    tm = min(ROW_TILE, seq)
    row, mod, vec = _row_specs(tm, d, seq // tm)
    wvec = pl.BlockSpec((1, w), lambda i: (0, 0))
    w_gate = _lru_gate_weights(w_a, w_x)
    return pl.pallas_call(
        functools.partial(_lru_kernel, seq // tm),
        grid=(t // tm,),
        in_specs=[row, mod, mod, mod, vec, vec, _resident(w_in.shape), _resident(conv_w.shape), wvec,
                  _resident(w_gate.shape), wvec, wvec, wvec, _resident(w_out.shape)],
        out_specs=row,
        out_shape=jax.ShapeDtypeStruct((t, d), F32),
        scratch_shapes=[pltpu.VMEM((V7X_SUBLANES + tm, w), F32), pltpu.VMEM((V7X_SUBLANES, w), F32)],
        compiler_params=_params(1),
        name="lru_mixer",
    )(x2, shift, scale, gate, g_pre, g_post, w_in.astype(BF16), conv_w, conv_b.reshape(1, w), w_gate,
      b_a.reshape(1, w), b_x.reshape(1, w), lam.reshape(1, w), w_out.astype(BF16))


def _fox_proj_kernel(tiles_per_batch, n_heads, x_ref, shift_ref, scale_ref, gpre_ref, wqkv_ref, wf_ref, bf_ref,
                     tri_ref, pq_ref, pk_ref, oneq_ref, onek_ref,
                     q_ref, k_ref, v_ref, augq_ref, augk_ref, carry_ref):
    d = x_ref.shape[1]
    n_pairs = q_ref.shape[1]

    @pl.when(pl.program_id(0) % tiles_per_batch == 0)
    def _():
        carry_ref[...] = jnp.zeros(carry_ref.shape, F32)

    h = _pre(x_ref[...], gpre_ref, scale_ref, shift_ref).astype(BF16)
    q = jnp.dot(h, wqkv_ref[:, :d], preferred_element_type=F32) * (FOX_HEAD_DIM ** -0.5 * LOG2E)
    k = jnp.dot(h, wqkv_ref[:, d:2 * d], preferred_element_type=F32)
    v = jnp.dot(h, wqkv_ref[:, 2 * d:], preferred_element_type=F32)
    for p in range(n_pairs):
        sl = slice(p * V7X_LANES, (p + 1) * V7X_LANES)
        q_ref[0, p] = q[:, sl].astype(BF16)
        k_ref[0, p] = k[:, sl].astype(BF16)
        v_ref[0, p] = v[:, sl].astype(BF16)

    f_logit = jnp.dot(h, wf_ref[...], preferred_element_type=F32) + bf_ref[...]
    log_f = jnp.minimum(f_logit, 0.0) - jnp.log1p(jnp.exp(-jnp.abs(f_logit)))
    lanes = lax.broadcasted_iota(jnp.int32, log_f.shape, 1)
    log_f = jnp.where(lanes < n_heads, log_f, 0.0)
    tri = tri_ref[...]
    cum = carry_ref[V7X_SUBLANES - 1:]
    for part in _split3(log_f):
        cum = cum + jnp.dot(tri, part, preferred_element_type=F32)
    carry_ref[...] = cum[cum.shape[0] - V7X_SUBLANES:]

    parts = jnp.concatenate(_split3(cum * LOG2E), axis=1)
    augq_ref[...] = (jnp.dot(parts, pq_ref[...], preferred_element_type=F32) + oneq_ref[...]).astype(BF16)
    augk_ref[...] = (jnp.dot(parts, pk_ref[...], preferred_element_type=F32) + onek_ref[...]).astype(BF16)


def _aug_constants(n_heads):
    assert n_heads * AUG_PER_HEAD <= V7X_LANES
    pq = np.zeros((3 * V7X_LANES, V7X_LANES), np.float32)
    pk = np.zeros((3 * V7X_LANES, V7X_LANES), np.float32)
    oneq = np.zeros((1, V7X_LANES), np.float32)
    onek = np.zeros((1, V7X_LANES), np.float32)
    for h in range(n_heads):
        for j in range(3):
            pq[j * V7X_LANES + h, AUG_PER_HEAD * h + j] = 1.0
            pk[j * V7X_LANES + h, AUG_PER_HEAD * h + 3 + j] = -1.0
            oneq[0, AUG_PER_HEAD * h + 3 + j] = 1.0
            onek[0, AUG_PER_HEAD * h + j] = 1.0
    return (jnp.asarray(pq, BF16), jnp.asarray(pk, BF16), jnp.asarray(oneq), jnp.asarray(onek))


def _fox_proj(x2, bsz, seq, shift, scale, g_pre, w_in, b_f):
    t, d = x2.shape
    n_heads = d // FOX_HEAD_DIM
    n_pairs = d // V7X_LANES
    tm = min(ROW_TILE, seq)
    tiles_per_batch = seq // tm
    row, mod, vec = _row_specs(tm, d, tiles_per_batch)
    w_qkv = w_in[:, :3 * d].astype(BF16)
    w_f = jnp.pad(w_in[:, 3 * d:], ((0, 0), (0, V7X_LANES - n_heads))).astype(BF16)
    b_f = jnp.pad(b_f, (0, V7X_LANES - n_heads)).reshape(1, V7X_LANES)
    tri = jnp.asarray(np.tril(np.ones((tm, tm), np.float32)), BF16)
    pq, pk, oneq, onek = _aug_constants(n_heads)
    pair_spec = pl.BlockSpec((1, n_pairs, tm, V7X_LANES),
                             lambda i: (i // tiles_per_batch, 0, i % tiles_per_batch, 0))
    aug_spec = pl.BlockSpec((tm, V7X_LANES), lambda i: (i, 0))
    pair_shape = jax.ShapeDtypeStruct((bsz, n_pairs, seq, V7X_LANES), BF16)
    aug_shape = jax.ShapeDtypeStruct((t, V7X_LANES), BF16)
    return pl.pallas_call(
        functools.partial(_fox_proj_kernel, tiles_per_batch, n_heads),
        grid=(t // tm,),
        in_specs=[row, mod, mod, vec, _resident(w_qkv.shape), _resident(w_f.shape), _resident(b_f.shape),
                  _resident(tri.shape), _resident(pq.shape), _resident(pk.shape), _resident(oneq.shape),
                  _resident(onek.shape)],
        out_specs=[pair_spec, pair_spec, pair_spec, aug_spec, aug_spec],
        out_shape=[pair_shape, pair_shape, pair_shape, aug_shape, aug_shape],
        scratch_shapes=[pltpu.VMEM((V7X_SUBLANES, V7X_LANES), F32)],
        compiler_params=_params(1),
        name="fox_proj",
    )(x2, shift, scale, g_pre, w_qkv, w_f, b_f, tri, pq, pk, oneq, onek)


def _fox_attn_kernel(x_ref, gate_ref, gpost_ref, q_ref, k_ref, v_ref, augq_ref, augk_ref, wout_ref,
                     o_ref, lhs_ref, s_ref, mx_ref, acc_ref, heads_ref):
    tq = x_ref.shape[0]
    n_pairs = q_ref.shape[1]
    group_pairs = s_ref.shape[0] // 2
    qi = pl.program_id(1)
    lanes = lax.broadcasted_iota(jnp.int32, (tq, V7X_LANES), 1)
    half = lanes // FOX_HEAD_DIM
    causal = (lax.broadcasted_iota(jnp.int32, (tq, tq), 1)
              <= lax.broadcasted_iota(jnp.int32, (tq, tq), 0))
    augq = augq_ref[...]
    contract_last = (((1,), (1,)), ((), ()))
    zero16 = jnp.zeros((), BF16)

    def key_rows(kb):
        return pl.ds(pl.multiple_of(kb * tq, tq), tq)

    def group(g):
        pairs = range(g * group_pairs, (g + 1) * group_pairs)
        for p in pairs:
            q_pair = q_ref[0, p]
            for e in range(2):
                h = 2 * p + e
                own_aug = (lanes >= AUG_PER_HEAD * h) & (lanes < AUG_PER_HEAD * (h + 1))
                lhs_ref[h] = jnp.concatenate(
                    [jnp.where(half == e, q_pair, zero16), jnp.where(own_aug, augq, zero16)], axis=1)
                mx_ref[h] = jnp.full((tq, V7X_LANES), -jnp.inf, F32)
                acc_ref[h] = jnp.zeros((tq, V7X_LANES), F32)

        def scores(kb, n_blocks, masked):
            for j in range(n_blocks):
                rows = key_rows(kb + j)
                aug_k = augk_ref[rows, :]
                for p in pairs:
                    rhs = jnp.concatenate([k_ref[0, p, rows, :], aug_k], axis=1)
                    for e in range(2):
                        h = 2 * p + e
                        s = lax.dot_general(lhs_ref[h], rhs, contract_last, preferred_element_type=F32)
                        if masked:
                            s = jnp.where(causal, s, -jnp.inf)
                        s_ref[h - 2 * pairs[0], kb + j] = s
                        mx_ref[h] = jnp.maximum(mx_ref[h], jnp.maximum(s[:, :V7X_LANES], s[:, V7X_LANES:]))

        def values(kb, n_blocks):
            for j in range(n_blocks):
                rows = key_rows(kb + j)
                for p in pairs:
                    v_pair = v_ref[0, p, rows, :]
                    for e in range(2):
                        h = 2 * p + e
                        m = mx_ref[h]
                        s = s_ref[h - 2 * pairs[0], kb + j]
                        prob = jnp.concatenate(
                            [jnp.exp2(s[:, :V7X_LANES] - m), jnp.exp2(s[:, V7X_LANES:] - m)], axis=1).astype(BF16)
                        v_own = jnp.where(half == e, v_pair, jnp.ones((), BF16))
                        acc_ref[h] += jnp.dot(prob, v_own, preferred_element_type=F32)

        scores(qi, 1, True)

        def scores_body(i, c):
            scores(2 * i, 2, False)
            return c

        lax.fori_loop(0, qi // 2, scores_body, 0)

        @pl.when(qi % 2 == 1)
        def _():
            scores(qi - 1, 1, False)

        for p in pairs:
            for e in range(2):
                h = 2 * p + e
                mx_ref[h] = jnp.broadcast_to(jnp.max(mx_ref[h], axis=1, keepdims=True), (tq, V7X_LANES))

        def values_body(i, c):
            values(2 * i, 2)
            return c

        lax.fori_loop(0, (qi + 1) // 2, values_body, 0)

        @pl.when(qi % 2 == 0)
        def _():
            values(qi, 1)

        for p in pairs:
            a_even = acc_ref[2 * p]
            a_odd = acc_ref[2 * p + 1]
            numer = jnp.where(half == 0, a_even, a_odd)
            denom = pltpu.roll(jnp.where(half == 0, a_odd, a_even), FOX_HEAD_DIM, axis=1)
            heads_ref[p] = (numer / denom).astype(BF16)

    for g in range(n_pairs // group_pairs):
        group(g)
    o = jnp.concatenate([heads_ref[p] for p in range(n_pairs)], axis=1)
    y = jnp.dot(o, wout_ref[...], preferred_element_type=F32)
    o_ref[...] = x_ref[...] + gate_ref[0] * (_rms(y) * gpost_ref[...])


def _fox_attn(x2, bsz, seq, gate, g_post, q, k, v, augq, augk, w_out):
    t, d = x2.shape
    n_pairs = d // V7X_LANES
    tq = min(ATTN_TILE, seq)
    nq = seq // tq
    group_pairs = min(ATTN_GROUP_PAIRS, n_pairs)
    assert n_pairs % group_pairs == 0
    row = pl.BlockSpec((tq, d), lambda b, i: (b * nq + i, 0))
    mod = pl.BlockSpec((1, 1, d), lambda b, i: (b, 0, 0))
    vec = pl.BlockSpec((1, d), lambda b, i: (0, 0))
    q_spec = pl.BlockSpec((1, n_pairs, tq, V7X_LANES), lambda b, i: (b, 0, i, 0))
    kv_spec = pl.BlockSpec((1, n_pairs, seq, V7X_LANES), lambda b, i: (b, 0, 0, 0))
    augq_spec = pl.BlockSpec((tq, V7X_LANES), lambda b, i: (b * nq + i, 0))
    augk_spec = pl.BlockSpec((seq, V7X_LANES), lambda b, i: (b, 0))
    return pl.pallas_call(
        _fox_attn_kernel,
        grid=(bsz, nq),
        in_specs=[row, mod, vec, q_spec, kv_spec, kv_spec, augq_spec, augk_spec, _resident(w_out.shape)],
        out_specs=row,
        out_shape=jax.ShapeDtypeStruct((t, d), F32),
        scratch_shapes=[pltpu.VMEM((2 * n_pairs, tq, 2 * V7X_LANES), BF16),
                        pltpu.VMEM((2 * group_pairs, nq, tq, tq), F32),
                        pltpu.VMEM((2 * n_pairs, tq, V7X_LANES), F32),
                        pltpu.VMEM((2 * n_pairs, tq, V7X_LANES), F32),
                        pltpu.VMEM((n_pairs, tq, V7X_LANES), BF16)],
        compiler_params=_params(2),
        name="fox_attn",
    )(x2, gate, g_post, q, k, v, augq, augk, w_out.astype(BF16))


def kernel(x, c, w_cond, b_cond, norm_pre, norm_post, w_ffn_in, w_ffn_out, fox_w_in, fox_b_f, fox_w_out, sconv_w_in, sconv_conv_w, sconv_w_out, lru_w_in, lru_conv_w, lru_conv_b, lru_w_a, lru_b_a, lru_w_x, lru_b_x, lru_lambda, lru_w_out):
    bsz, seq, d = x.shape
    depth = w_cond.shape[0]
    assert d % V7X_LANES == 0 and V7X_LANES == 2 * FOX_HEAD_DIM
    assert seq % min(ROW_TILE, seq) == 0 and seq % min(ATTN_TILE, seq) == 0

    mod = _modulation(c, w_cond, b_cond).reshape(depth, bsz, N_SUB, 3, 1, d)
    x2 = x.reshape(bsz * seq, d)
    w_ffn_in16 = w_ffn_in.astype(BF16)
    w_ffn_out16 = w_ffn_out.astype(BF16)

    for i in range(depth):
        def mods(s):
            return mod[i, :, s, 0], mod[i, :, s, 1], mod[i, :, s, 2]

        def gains(s):
            return norm_pre[i, s].reshape(1, d), norm_post[i, s].reshape(1, d)

        shift, scale, gate = mods(0)
        x2 = _ffn(x2, seq, shift, scale, gate, *gains(0), w_ffn_in16, w_ffn_out16, (i, 0))

        shift, scale, gate = mods(1)
        g_pre, g_post = gains(1)
        kind, j = i % N_MIXERS, i // N_MIXERS
        if kind == 0:
            q, k, v, augq, augk = _fox_proj(x2, bsz, seq, shift, scale, g_pre, fox_w_in[j], fox_b_f[j])
            x2 = _fox_attn(x2, bsz, seq, gate, g_post, q, k, v, augq, augk, fox_w_out[j])
        elif kind == 1:
            x2 = _sconv(x2, seq, shift, scale, gate, g_pre, g_post, sconv_w_in[j], sconv_conv_w[j],
                        sconv_w_out[j])
        else:
            x2 = _lru(x2, seq, shift, scale, gate, g_pre, g_post, lru_w_in[j], lru_conv_w[j],
                      lru_conv_b[j], lru_w_a[j], lru_b_a[j], lru_w_x[j], lru_b_x[j], lru_lambda[j],
                      lru_w_out[j])

        shift, scale, gate = mods(2)
        x2 = _ffn(x2, seq, shift, scale, gate, *gains(2), w_ffn_in16, w_ffn_out16, (i, 1))

    return x2.reshape(bsz, seq, d)
```

```python
import functools

import numpy as np
import jax
import jax.numpy as jnp
from jax import lax
from jax.experimental import pallas as pl
from jax.experimental.pallas import tpu as pltpu

N_SUB = 3
N_MIXERS = 3
FFN_RES_WEIGHT = 0.5
RMS_EPS = 1e-6
FOX_HEAD_DIM = 64
LRU_C = 8.0
LOG2E = 1.4426950408889634

V7X_LANES = 128
V7X_SUBLANES = 8
V7X_MXU_DIM = 256
V7X_VMEM_BYTES = 64 * 1024 * 1024
VMEM_LIMIT_BYTES = V7X_VMEM_BYTES - 8 * 1024 * 1024

ROW_TILE = 512
ATTN_TILE = 256
FFN_CHUNK = V7X_MXU_DIM
FFN_ROW_TILE = 1024
FFN_SUBTILE = 512
AUG_PER_HEAD = 6
ATTN_GROUP_PAIRS = 4
MOD_COL_TILE = 1536

BF16 = jnp.bfloat16
F32 = jnp.float32


def _params(n_axes):
    return pltpu.CompilerParams(dimension_semantics=("arbitrary",) * n_axes,
                                vmem_limit_bytes=VMEM_LIMIT_BYTES)


def _resident(shape):
    return pl.BlockSpec(shape, lambda *_: (0,) * len(shape), pipeline_mode=pl.Buffered(1))


def _rms(x):
    return x * lax.rsqrt(jnp.mean(x * x, axis=-1, keepdims=True) + RMS_EPS)


def _pre(x, gpre_ref, scale_ref, shift_ref):
    return _rms(x) * (gpre_ref[...] * (1.0 + scale_ref[0])) + shift_ref[0]


def _residual(x, y, gate_ref, gpost_ref, weight=1.0):
    return x + _rms(y) * ((weight * gate_ref[0]) * gpost_ref[...])


def _split3(x):
    hi = x.astype(BF16)
    r1 = x - hi.astype(F32)
    mid = r1.astype(BF16)
    lo = (r1 - mid.astype(F32)).astype(BF16)
    return hi, mid, lo


def _mod_kernel(c_ref, w_ref, b_ref, o_ref):
    c = c_ref[...]
    c_act = (c * jax.nn.sigmoid(c)).astype(BF16)
    o_ref[0] = jnp.dot(c_act, w_ref[0].astype(BF16), preferred_element_type=F32) + b_ref[0]


def _modulation(c, w_cond, b_cond):
    depth, d, n = w_cond.shape
    bsz = c.shape[0]
    tn = MOD_COL_TILE if n % MOD_COL_TILE == 0 else n
    return pl.pallas_call(
        _mod_kernel,
        grid=(depth, n // tn),
        in_specs=[
            pl.BlockSpec((bsz, d), lambda i, j: (0, 0)),
            pl.BlockSpec((1, d, tn), lambda i, j: (i, 0, j)),
            pl.BlockSpec((1, 1, tn), lambda i, j: (i, 0, j)),
        ],
        out_specs=pl.BlockSpec((1, bsz, tn), lambda i, j: (i, 0, j)),
        out_shape=jax.ShapeDtypeStruct((depth, bsz, n), F32),
        compiler_params=_params(2),
        name="adaln_mod",
    )(c, w_cond, b_cond.reshape(depth, 1, n))


def _ffn_kernel(x_ref, shift_ref, scale_ref, gate_ref, gpre_ref, gpost_ref, win_ref, wout_ref,
                o_ref, act_ref):
    f = wout_ref.shape[0]
    fc = FFN_CHUNK
    tm = x_ref.shape[0]
    for r in range(0, tm, FFN_SUBTILE):
        rows = slice(r, r + FFN_SUBTILE)
        x = x_ref[rows, :]
        h = _pre(x, gpre_ref, scale_ref, shift_ref).astype(BF16)
        for c in range(f // fc):
            g = jnp.dot(h, win_ref[:, c * fc:(c + 1) * fc], preferred_element_type=F32)
            u = jnp.dot(h, win_ref[:, f + c * fc:f + (c + 1) * fc], preferred_element_type=F32)
            act_ref[rows, c * fc:(c + 1) * fc] = (g * jax.nn.sigmoid(g) * u).astype(BF16)
        y = jnp.dot(act_ref[rows, :], wout_ref[...], preferred_element_type=F32)
        o_ref[rows, :] = _residual(x, y, gate_ref, gpost_ref, FFN_RES_WEIGHT)


def _row_specs(tm, d, tiles_per_batch):
    row = pl.BlockSpec((tm, d), lambda i: (i, 0))
    mod = pl.BlockSpec((1, 1, d), lambda i: (i // tiles_per_batch, 0, 0))
    vec = pl.BlockSpec((1, d), lambda i: (0, 0))
    return row, mod, vec


def _stacked_resident(stacked, index):
    lead = len(index)
    shape = (None,) * lead + tuple(stacked.shape[lead:])
    return pl.BlockSpec(shape, lambda *_: tuple(index) + (0,) * (stacked.ndim - lead),
                        pipeline_mode=pl.Buffered(1))


def _ffn(x2, seq, shift, scale, gate, g_pre, g_post, w_in, w_out, index):
    t, d = x2.shape
    tm = min(FFN_ROW_TILE, seq)
    f = w_out.shape[-2]
    assert f % FFN_CHUNK == 0 and tm % FFN_SUBTILE == 0
    row, mod, vec = _row_specs(tm, d, seq // tm)
    return pl.pallas_call(
        _ffn_kernel,
        grid=(t // tm,),
        in_specs=[row, mod, mod, mod, vec, vec, _stacked_resident(w_in, index),
                  _stacked_resident(w_out, index)],
        out_specs=row,
        out_shape=jax.ShapeDtypeStruct((t, d), F32),
        scratch_shapes=[pltpu.VMEM((tm, f), BF16)],
        compiler_params=_params(1),
        name="ffn",
    )(x2, shift, scale, gate, g_pre, g_post, w_in, w_out)


def _causal_conv(u, hist_ref, w_ref):
    tm = u.shape[0]
    kw = w_ref.shape[0]
    hist_ref[V7X_SUBLANES:, :] = u
    out = w_ref[kw - 1:kw] * u
    for k in range(1, kw):
        out = out + w_ref[kw - 1 - k:kw - k] * hist_ref[V7X_SUBLANES - k:V7X_SUBLANES - k + tm, :]
    hist_ref[:V7X_SUBLANES, :] = u[tm - V7X_SUBLANES:]
    return out


def _sconv_kernel(tiles_per_batch, x_ref, shift_ref, scale_ref, gate_ref, gpre_ref, gpost_ref,
                  win_ref, convw_ref, wout_ref, o_ref, hist_ref):
    d = x_ref.shape[1]

    @pl.when(pl.program_id(0) % tiles_per_batch == 0)
    def _():
        hist_ref[:V7X_SUBLANES, :] = jnp.zeros((V7X_SUBLANES, d), F32)

    x = x_ref[...]
    h = _pre(x, gpre_ref, scale_ref, shift_ref).astype(BF16)
    c_gate = jnp.dot(h, win_ref[:, d:2 * d], preferred_element_type=F32)
    xv = jnp.dot(h, win_ref[:, 2 * d:], preferred_element_type=F32)
    u = c_gate * xv
    conv = _causal_conv(u, hist_ref, convw_ref)
    b_gate = jnp.dot(h, win_ref[:, :d], preferred_element_type=F32)
    y = jnp.dot((b_gate * conv).astype(BF16), wout_ref[...], preferred_element_type=F32)
    o_ref[...] = _residual(x, y, gate_ref, gpost_ref)


def _sconv(x2, seq, shift, scale, gate, g_pre, g_post, w_in, conv_w, w_out):
    t, d = x2.shape
    tm = min(ROW_TILE, seq)
    row, mod, vec = _row_specs(tm, d, seq // tm)
    return pl.pallas_call(
        functools.partial(_sconv_kernel, seq // tm),
        grid=(t // tm,),
        in_specs=[row, mod, mod, mod, vec, vec, _resident(w_in.shape), _resident(conv_w.shape),
                  _resident(w_out.shape)],
        out_specs=row,
        out_shape=jax.ShapeDtypeStruct((t, d), F32),
        scratch_shapes=[pltpu.VMEM((V7X_SUBLANES + tm, d), F32)],
        compiler_params=_params(1),
        name="sconv_mixer",
    )(x2, shift, scale, gate, g_pre, g_post, w_in.astype(BF16), conv_w, w_out.astype(BF16))


def _linear_scan(a, b, h_prev):
    n, c = a.shape
    grouped = (n // V7X_SUBLANES, V7X_SUBLANES, c)
    sub = lax.broadcasted_iota(jnp.int32, grouped, 1)
    a = a.reshape(grouped)
    b = b.reshape(grouped)
    d = 1
    while d < V7X_SUBLANES:
        keep = sub >= d
        b = jnp.where(keep, b + a * pltpu.roll(b, d, axis=1), b)
        a = jnp.where(keep, a * pltpu.roll(a, d, axis=1), a)
        d *= 2
    a = a.reshape(n, c)
    b = b.reshape(n, c)
    groups = []
    for r in range(0, n, V7X_SUBLANES):
        h_group = a[r:r + V7X_SUBLANES] * h_prev + b[r:r + V7X_SUBLANES]
        groups.append(h_group)
        h_prev = h_group[V7X_SUBLANES - 1:]
    return jnp.concatenate(groups, axis=0)


def _lru_kernel(tiles_per_batch, x_ref, shift_ref, scale_ref, gate_ref, gpre_ref, gpost_ref,
                win_ref, convw_ref, convb_ref, wgate_ref, ba_ref, bx_ref, lam_ref, wout_ref,
                o_ref, xhist_ref, hcarry_ref):
    tm = x_ref.shape[0]
    w = wout_ref.shape[0]
    n_groups, gk, two_gn = wgate_ref.shape
    gn = two_gn // 2

    @pl.when(pl.program_id(0) % tiles_per_batch == 0)
    def _():
        xhist_ref[:V7X_SUBLANES, :] = jnp.zeros((V7X_SUBLANES, w), F32)
        hcarry_ref[...] = jnp.zeros(hcarry_ref.shape, F32)

    x = x_ref[...]
    h = _pre(x, gpre_ref, scale_ref, shift_ref).astype(BF16)
    xraw = jnp.dot(h, win_ref[:, w:], preferred_element_type=F32)
    xb = _causal_conv(xraw, xhist_ref, convw_ref) + convb_ref[...]

    xb16 = xb.astype(BF16)
    r_parts, i_parts = [], []
    for g in range(n_groups):
        ri = jnp.dot(xb16[:, g * gk:(g + 1) * gk], wgate_ref[g], preferred_element_type=F32)
        r_parts.append(ri[:, :gn])
        i_parts.append(ri[:, gn:])
    r = jax.nn.sigmoid(jnp.concatenate(r_parts, axis=1) + ba_ref[...])
    i = jax.nn.sigmoid(jnp.concatenate(i_parts, axis=1) + bx_ref[...])

    neg_lam = -lam_ref[...]
    softplus = jnp.maximum(neg_lam, 0.0) + jnp.log1p(jnp.exp(-jnp.abs(neg_lam)))
    log_a = (-LRU_C * softplus) * r
    a = jnp.exp(log_a)
    mult = jnp.sqrt(-jnp.tanh(log_a) * (1.0 + a * a))
    hs = _linear_scan(a, mult * (i * xb), hcarry_ref[V7X_SUBLANES - 1:])
    hcarry_ref[...] = hs[tm - V7X_SUBLANES:]

    gate_br = jnp.dot(h, win_ref[:, :w], preferred_element_type=F32)
    y = jnp.dot((hs * jax.nn.gelu(gate_br)).astype(BF16), wout_ref[...], preferred_element_type=F32)
    o_ref[...] = _residual(x, y, gate_ref, gpost_ref)


def _lru_gate_weights(w_a, w_x):
    nb, bd, _ = w_a.shape
    per = V7X_MXU_DIM // bd
    assert nb % per == 0

    def dense(wb):
        wg = wb.reshape(nb // per, per, bd, bd)
        eye = jnp.eye(per, dtype=wb.dtype)
        return jnp.einsum("gpij,pq->gpiqj", wg, eye).reshape(nb // per, per * bd, per * bd)

    return jnp.concatenate([dense(w_a), dense(w_x)], axis=-1).astype(BF16)


def _lru(x2, seq, shift, scale, gate, g_pre, g_post, w_in, conv_w, conv_b, w_a, b_a, w_x, b_x, lam,
         w_out):
    t, d = x2.shape
    w = w_out.shape[0]
    tm = min(ROW_TILE, seq)
    row, mod, vec = _row_specs(tm, d, seq // tm)
    wvec = pl.BlockSpec((1, w), lambda i: (0, 0))
    w_gate = _lru_gate_weights(w_a, w_x)
    return pl.pallas_call(
        functools.partial(_lru_kernel, seq // tm),
        grid=(t // tm,),
        in_specs=[row, mod, mod, mod, vec, vec, _resident(w_in.shape), _resident(conv_w.shape), wvec,
                  _resident(w_gate.shape), wvec, wvec, wvec, _resident(w_out.shape)],
        out_specs=row,
        out_shape=jax.ShapeDtypeStruct((t, d), F32),
        scratch_shapes=[pltpu.VMEM((V7X_SUBLANES + tm, w), F32), pltpu.VMEM((V7X_SUBLANES, w), F32)],
        compiler_params=_params(1),
        name="lru_mixer",
    )(x2, shift, scale, gate, g_pre, g_post, w_in.astype(BF16), conv_w, conv_b.reshape(1, w), w_gate,
      b_a.reshape(1, w), b_x.reshape(1, w), lam.reshape(1, w), w_out.astype(BF16))


def _fox_proj_kernel(tiles_per_batch, n_heads, x_ref, shift_ref, scale_ref, gpre_ref, wqk_ref, wvt_ref, wf_ref,
                     bf_ref, tri_ref, pq_ref, pk_ref, oneq_ref, onek_ref,
                     q_ref, k_ref, vt_ref, augq_ref, augk_ref, carry_ref):
    d = x_ref.shape[1]
    n_pairs = q_ref.shape[1]
    tk = vt_ref.shape[4]

    @pl.when(pl.program_id(0) % tiles_per_batch == 0)
    def _():
        carry_ref[...] = jnp.zeros(carry_ref.shape, F32)

    h = _pre(x_ref[...], gpre_ref, scale_ref, shift_ref).astype(BF16)
    q = jnp.dot(h, wqk_ref[:, :d], preferred_element_type=F32) * (FOX_HEAD_DIM ** -0.5 * LOG2E)
    k = jnp.dot(h, wqk_ref[:, d:], preferred_element_type=F32)
    vt = lax.dot_general(wvt_ref[...], h, (((1,), (1,)), ((), ())), preferred_element_type=F32)
    for p in range(n_pairs):
        sl = slice(p * V7X_LANES, (p + 1) * V7X_LANES)
        q_ref[0, p] = q[:, sl].astype(BF16)
        k_ref[0, p] = k[:, sl].astype(BF16)
        for j in range(vt_ref.shape[2]):
            vt_ref[0, p, j] = vt[sl, j * tk:(j + 1) * tk].astype(BF16)

    f_logit = jnp.dot(h, wf_ref[...], preferred_element_type=F32) + bf_ref[...]
    log_f = jnp.minimum(f_logit, 0.0) - jnp.log1p(jnp.exp(-jnp.abs(f_logit)))
    lanes = lax.broadcasted_iota(jnp.int32, log_f.shape, 1)
    log_f = jnp.where(lanes < n_heads, log_f, 0.0)
    tri = tri_ref[...]
    cum = carry_ref[V7X_SUBLANES - 1:]
    for part in _split3(log_f):
        cum = cum + jnp.dot(tri, part, preferred_element_type=F32)
    carry_ref[...] = cum[cum.shape[0] - V7X_SUBLANES:]

    parts = jnp.concatenate(_split3(cum * LOG2E), axis=1)
    augq_ref[...] = (jnp.dot(parts, pq_ref[...], preferred_element_type=F32) + oneq_ref[...]).astype(BF16)
    augk_ref[...] = (jnp.dot(parts, pk_ref[...], preferred_element_type=F32) + onek_ref[...]).astype(BF16)


def _aug_constants(n_heads):
    assert n_heads * AUG_PER_HEAD <= V7X_LANES
    pq = np.zeros((3 * V7X_LANES, V7X_LANES), np.float32)
    pk = np.zeros((3 * V7X_LANES, V7X_LANES), np.float32)
    oneq = np.zeros((1, V7X_LANES), np.float32)
    onek = np.zeros((1, V7X_LANES), np.float32)
    for h in range(n_heads):
        for j in range(3):
            pq[j * V7X_LANES + h, AUG_PER_HEAD * h + j] = 1.0
            pk[j * V7X_LANES + h, AUG_PER_HEAD * h + 3 + j] = -1.0
            oneq[0, AUG_PER_HEAD * h + 3 + j] = 1.0
            onek[0, AUG_PER_HEAD * h + j] = 1.0
    return (jnp.asarray(pq, BF16), jnp.asarray(pk, BF16), jnp.asarray(oneq), jnp.asarray(onek))


def _fox_proj(x2, bsz, seq, shift, scale, g_pre, w_in, b_f):
    t, d = x2.shape
    n_heads = d // FOX_HEAD_DIM
    n_pairs = d // V7X_LANES
    tm = min(ROW_TILE, seq)
    tiles_per_batch = seq // tm
    row, mod, vec = _row_specs(tm, d, tiles_per_batch)
    tk = min(ATTN_TILE, seq)
    assert tm % tk == 0
    w_qk = w_in[:, :2 * d].astype(BF16)
    w_vt = w_in[:, 2 * d:3 * d].T.astype(BF16)
    w_f = jnp.pad(w_in[:, 3 * d:], ((0, 0), (0, V7X_LANES - n_heads))).astype(BF16)
    b_f = jnp.pad(b_f, (0, V7X_LANES - n_heads)).reshape(1, V7X_LANES)
    tri = jnp.asarray(np.tril(np.ones((tm, tm), np.float32)), BF16)
    pq, pk, oneq, onek = _aug_constants(n_heads)
    pair_spec = pl.BlockSpec((1, n_pairs, tm, V7X_LANES),
                             lambda i: (i // tiles_per_batch, 0, i % tiles_per_batch, 0))
    vt_spec = pl.BlockSpec((1, n_pairs, tm // tk, V7X_LANES, tk),
                           lambda i: (i // tiles_per_batch, 0, i % tiles_per_batch, 0, 0))
    aug_spec = pl.BlockSpec((tm, V7X_LANES), lambda i: (i, 0))
    pair_shape = jax.ShapeDtypeStruct((bsz, n_pairs, seq, V7X_LANES), BF16)
    vt_shape = jax.ShapeDtypeStruct((bsz, n_pairs, seq // tk, V7X_LANES, tk), BF16)
    aug_shape = jax.ShapeDtypeStruct((t, V7X_LANES), BF16)
    return pl.pallas_call(
        functools.partial(_fox_proj_kernel, tiles_per_batch, n_heads),
        grid=(t // tm,),
        in_specs=[row, mod, mod, vec, _resident(w_qk.shape), _resident(w_vt.shape), _resident(w_f.shape),
                  _resident(b_f.shape), _resident(tri.shape), _resident(pq.shape), _resident(pk.shape),
                  _resident(oneq.shape), _resident(onek.shape)],
        out_specs=[pair_spec, pair_spec, vt_spec, aug_spec, aug_spec],
        out_shape=[pair_shape, pair_shape, vt_shape, aug_shape, aug_shape],
        scratch_shapes=[pltpu.VMEM((V7X_SUBLANES, V7X_LANES), F32)],
        compiler_params=_params(1),
        name="fox_proj",
    )(x2, shift, scale, g_pre, w_qk, w_vt, w_f, b_f, tri, pq, pk, oneq, onek)


def _fox_attn_kernel(x_ref, gate_ref, gpost_ref, q_ref, k_ref, vt_ref, augq_ref, augk_ref, wout_ref,
                     o_ref, lhs_ref, s_ref, mx_ref, acc_ref, heads_ref):
    tq = x_ref.shape[0]
    n_pairs = q_ref.shape[1]
    group_pairs = s_ref.shape[0] // 2
    qi = pl.program_id(1)
    lanes = lax.broadcasted_iota(jnp.int32, (tq, V7X_LANES), 1)
    half = lanes // FOX_HEAD_DIM
    causal_t = (lax.broadcasted_iota(jnp.int32, (tq, tq), 0)
                <= lax.broadcasted_iota(jnp.int32, (tq, tq), 1))
    augq = augq_ref[...]
    contract_last = (((1,), (1,)), ((), ()))
    zero16 = jnp.zeros((), BF16)
    ones_rows = jnp.ones((FOX_HEAD_DIM, tq), BF16)
    n_slabs = tq // V7X_SUBLANES

    def key_rows(kb):
        return pl.ds(pl.multiple_of(kb * tq, tq), tq)

    def group_pairs_of(g):
        return range(g * group_pairs, (g + 1) * group_pairs)

    def stage(g):
        for p in group_pairs_of(g):
            q_pair = q_ref[0, p]
            for e in range(2):
                h = 2 * p + e
                own_aug = (lanes >= AUG_PER_HEAD * h) & (lanes < AUG_PER_HEAD * (h + 1))
                lhs_ref[h] = jnp.concatenate(
                    [jnp.where(half == e, q_pair, zero16), jnp.where(own_aug, augq, zero16)], axis=1)
                mx_ref[h] = jnp.full((V7X_SUBLANES, tq), -jnp.inf, F32)
                acc_ref[h] = jnp.zeros((V7X_LANES, tq), F32)

    def scores(g, kb, n_blocks, masked):
        pairs = group_pairs_of(g)
        for j in range(n_blocks):
            rows = key_rows(kb + j)
            aug_k = augk_ref[rows, :]
            for p in pairs:
                keys = jnp.concatenate([k_ref[0, p, rows, :], aug_k], axis=1)
                for e in range(2):
                    h = 2 * p + e
                    s = lax.dot_general(keys, lhs_ref[h], contract_last, preferred_element_type=F32)
                    if masked:
                        s = jnp.where(causal_t, s, -jnp.inf)
                    s_ref[h - 2 * pairs[0], kb + j] = s
                    slabs = [s[i * V7X_SUBLANES:(i + 1) * V7X_SUBLANES] for i in range(n_slabs)]
                    while len(slabs) > 1:
                        slabs = [jnp.maximum(a, b) for a, b in zip(slabs[0::2], slabs[1::2])]
                    mx_ref[h] = jnp.maximum(mx_ref[h], slabs[0])

    def column_max(g):
        for p in group_pairs_of(g):
            for e in range(2):
                h = 2 * p + e
                mx_ref[h] = jnp.broadcast_to(jnp.max(mx_ref[h], axis=0, keepdims=True), (V7X_SUBLANES, tq))

    def values(g, kb, n_blocks):
        pairs = group_pairs_of(g)
        for p in pairs:
            vt_pair = jnp.concatenate([vt_ref[0, p, kb + j] for j in range(n_blocks)], axis=1)
            ones = jnp.concatenate([ones_rows] * n_blocks, axis=1)
            for e in range(2):
                h = 2 * p + e
                m = pltpu.repeat(mx_ref[h], n_slabs, axis=0)
                prob = jnp.concatenate(
                    [jnp.exp2(s_ref[h - 2 * pairs[0], kb + j] - m).astype(BF16) for j in range(n_blocks)],
                    axis=0)
                vt_own = (jnp.concatenate([vt_pair[:FOX_HEAD_DIM], ones], axis=0) if e == 0 else
                          jnp.concatenate([ones, vt_pair[FOX_HEAD_DIM:]], axis=0))
                acc_ref[h] += jnp.dot(vt_own, prob, preferred_element_type=F32)

    def finalize(g):
        for p in group_pairs_of(g):
            a_even = acc_ref[2 * p]
            a_odd = acc_ref[2 * p + 1]
            numer = jnp.concatenate([a_even[:FOX_HEAD_DIM], a_odd[FOX_HEAD_DIM:]], axis=0)
            denom = jnp.concatenate([a_even[FOX_HEAD_DIM:], a_odd[:FOX_HEAD_DIM]], axis=0)
            heads_ref[p] = (numer / denom).T.astype(BF16)

    for g in range(n_pairs // group_pairs):
        stage(g)
        scores(g, qi, 1, True)

        def scores_body(i, c, g=g):
            scores(g, 2 * i, 2, False)
            return c

        lax.fori_loop(0, qi // 2, scores_body, 0)

        @pl.when(qi % 2 == 1)
        def _(g=g):
            scores(g, qi - 1, 1, False)

        column_max(g)

        def values_body(i, c, g=g):
            values(g, 2 * i, 2)
            return c

        lax.fori_loop(0, (qi + 1) // 2, values_body, 0)

        @pl.when(qi % 2 == 0)
        def _(g=g):
            values(g, qi, 1)

        finalize(g)

    o = jnp.concatenate([heads_ref[p] for p in range(n_pairs)], axis=1)
    y = jnp.dot(o, wout_ref[...], preferred_element_type=F32)
    o_ref[...] = _residual(x_ref[...], y, gate_ref, gpost_ref)


def _fox_attn(x2, bsz, seq, gate, g_post, q, k, vt, augq, augk, w_out):
    t, d = x2.shape
    n_pairs = d // V7X_LANES
    tq = min(ATTN_TILE, seq)
    nq = seq // tq
    group_pairs = min(ATTN_GROUP_PAIRS, n_pairs)
    assert n_pairs % group_pairs == 0
    row = pl.BlockSpec((tq, d), lambda b, i: (b * nq + i, 0))
    mod = pl.BlockSpec((1, 1, d), lambda b, i: (b, 0, 0))
    vec = pl.BlockSpec((1, d), lambda b, i: (0, 0))
    q_spec = pl.BlockSpec((1, n_pairs, tq, V7X_LANES), lambda b, i: (b, 0, i, 0))
    k_spec = pl.BlockSpec((1, n_pairs, seq, V7X_LANES), lambda b, i: (b, 0, 0, 0))
    vt_spec = pl.BlockSpec((1, n_pairs, nq, V7X_LANES, tq), lambda b, i: (b, 0, 0, 0, 0))
    augq_spec = pl.BlockSpec((tq, V7X_LANES), lambda b, i: (b * nq + i, 0))
    augk_spec = pl.BlockSpec((seq, V7X_LANES), lambda b, i: (b, 0))
    return pl.pallas_call(
        _fox_attn_kernel,
        grid=(bsz, nq),
        in_specs=[row, mod, vec, q_spec, k_spec, vt_spec, augq_spec, augk_spec, _resident(w_out.shape)],
        out_specs=row,
        out_shape=jax.ShapeDtypeStruct((t, d), F32),
        scratch_shapes=[pltpu.VMEM((2 * n_pairs, tq, 2 * V7X_LANES), BF16),
                        pltpu.VMEM((2 * group_pairs, nq, tq, tq), F32),
                        pltpu.VMEM((2 * n_pairs, V7X_SUBLANES, tq), F32),
                        pltpu.VMEM((2 * n_pairs, V7X_LANES, tq), F32),
                        pltpu.VMEM((n_pairs, tq, V7X_LANES), BF16)],
        compiler_params=_params(2),
        name="fox_attn",
    )(x2, gate, g_post, q, k, vt, augq, augk, w_out.astype(BF16))


def kernel(x, c, w_cond, b_cond, norm_pre, norm_post, w_ffn_in, w_ffn_out, fox_w_in, fox_b_f, fox_w_out, sconv_w_in, sconv_conv_w, sconv_w_out, lru_w_in, lru_conv_w, lru_conv_b, lru_w_a, lru_b_a, lru_w_x, lru_b_x, lru_lambda, lru_w_out):
    bsz, seq, d = x.shape
    depth = w_cond.shape[0]
    assert d % V7X_LANES == 0 and V7X_LANES == 2 * FOX_HEAD_DIM
    assert seq % min(ROW_TILE, seq) == 0 and seq % min(ATTN_TILE, seq) == 0

    mod = _modulation(c, w_cond, b_cond).reshape(depth, bsz, N_SUB, 3, 1, d)
    x2 = x.reshape(bsz * seq, d)
    w_ffn_in16 = w_ffn_in.astype(BF16)
    w_ffn_out16 = w_ffn_out.astype(BF16)

    for i in range(depth):
        def mods(s):
            return mod[i, :, s, 0], mod[i, :, s, 1], mod[i, :, s, 2]

        def gains(s):
            return norm_pre[i, s].reshape(1, d), norm_post[i, s].reshape(1, d)

        shift, scale, gate = mods(0)
        x2 = _ffn(x2, seq, shift, scale, gate, *gains(0), w_ffn_in16, w_ffn_out16, (i, 0))

        shift, scale, gate = mods(1)
        g_pre, g_post = gains(1)
        kind, j = i % N_MIXERS, i // N_MIXERS
        if kind == 0:
            q, k, vt, augq, augk = _fox_proj(x2, bsz, seq, shift, scale, g_pre, fox_w_in[j], fox_b_f[j])
            x2 = _fox_attn(x2, bsz, seq, gate, g_post, q, k, vt, augq, augk, fox_w_out[j])
        elif kind == 1:
            x2 = _sconv(x2, seq, shift, scale, gate, g_pre, g_post, sconv_w_in[j], sconv_conv_w[j],
                        sconv_w_out[j])
        else:
            x2 = _lru(x2, seq, shift, scale, gate, g_pre, g_post, lru_w_in[j], lru_conv_w[j],
                      lru_conv_b[j], lru_w_a[j], lru_b_a[j], lru_w_x[j], lru_b_x[j], lru_lambda[j],
                      lru_w_out[j])

        shift, scale, gate = mods(2)
        x2 = _ffn(x2, seq, shift, scale, gate, *gains(2), w_ffn_in16, w_ffn_out16, (i, 1))

    return x2.reshape(bsz, seq, d)
```

```python
import functools

import numpy as np
import jax
import jax.numpy as jnp
from jax import lax
from jax.experimental import pallas as pl
from jax.experimental.pallas import tpu as pltpu

N_SUB = 3
N_MIXERS = 3
FFN_RES_WEIGHT = 0.5
RMS_EPS = 1e-6
FOX_HEAD_DIM = 64
LRU_C = 8.0
LOG2E = 1.4426950408889634

V7X_LANES = 128
V7X_SUBLANES = 8
V7X_MXU_DIM = 256
V7X_VMEM_BYTES = 64 * 1024 * 1024
VMEM_LIMIT_BYTES = V7X_VMEM_BYTES - 8 * 1024 * 1024

ROW_TILE = 512
ATTN_TILE = 256
FFN_CHUNK = V7X_MXU_DIM
FFN_ROW_TILE = 1024
FFN_SUBTILE = 512
AUG_PER_HEAD = 6
ATTN_GROUP_PAIRS = 8
MOD_COL_TILE = 1536

BF16 = jnp.bfloat16
F32 = jnp.float32


def _params(n_axes):
    return pltpu.CompilerParams(dimension_semantics=("arbitrary",) * n_axes,
                                vmem_limit_bytes=VMEM_LIMIT_BYTES)


def _resident(shape):
    return pl.BlockSpec(shape, lambda *_: (0,) * len(shape), pipeline_mode=pl.Buffered(1))


def _rms(x):
    return x * lax.rsqrt(jnp.mean(x * x, axis=-1, keepdims=True) + RMS_EPS)


def _pre(x, gpre_ref, scale_ref, shift_ref):
    return _rms(x) * (gpre_ref[...] * (1.0 + scale_ref[0])) + shift_ref[0]


def _residual(x, y, gate_ref, gpost_ref, weight=1.0):
    return x + _rms(y) * ((weight * gate_ref[0]) * gpost_ref[...])


def _split3(x):
    hi = x.astype(BF16)
    r1 = x - hi.astype(F32)
    mid = r1.astype(BF16)
    lo = (r1 - mid.astype(F32)).astype(BF16)
    return hi, mid, lo


def _mod_kernel(c_ref, w_ref, b_ref, o_ref):
    c = c_ref[...]
    c_act = (c * jax.nn.sigmoid(c)).astype(BF16)
    o_ref[0] = jnp.dot(c_act, w_ref[0].astype(BF16), preferred_element_type=F32) + b_ref[0]


def _modulation(c, w_cond, b_cond):
    depth, d, n = w_cond.shape
    bsz = c.shape[0]
    tn = MOD_COL_TILE if n % MOD_COL_TILE == 0 else n
    return pl.pallas_call(
        _mod_kernel,
        grid=(depth, n // tn),
        in_specs=[
            pl.BlockSpec((bsz, d), lambda i, j: (0, 0)),
            pl.BlockSpec((1, d, tn), lambda i, j: (i, 0, j)),
            pl.BlockSpec((1, 1, tn), lambda i, j: (i, 0, j)),
        ],
        out_specs=pl.BlockSpec((1, bsz, tn), lambda i, j: (i, 0, j)),
        out_shape=jax.ShapeDtypeStruct((depth, bsz, n), F32),
        compiler_params=_params(2),
        name="adaln_mod",
    )(c, w_cond, b_cond.reshape(depth, 1, n))


def _ffn_kernel(x_ref, shift_ref, scale_ref, gate_ref, gpre_ref, gpost_ref, win_ref, wout_ref,
                o_ref, act_ref):
    f = wout_ref.shape[0]
    fc = FFN_CHUNK
    tm = x_ref.shape[0]
    for r in range(0, tm, FFN_SUBTILE):
        rows = slice(r, r + FFN_SUBTILE)
        x = x_ref[rows, :]
        h = _pre(x, gpre_ref, scale_ref, shift_ref).astype(BF16)
        for c in range(f // fc):
            g = jnp.dot(h, win_ref[:, c * fc:(c + 1) * fc], preferred_element_type=F32)
            u = jnp.dot(h, win_ref[:, f + c * fc:f + (c + 1) * fc], preferred_element_type=F32)
            act_ref[rows, c * fc:(c + 1) * fc] = (g * jax.nn.sigmoid(g) * u).astype(BF16)
        y = jnp.dot(act_ref[rows, :], wout_ref[...], preferred_element_type=F32)
        o_ref[rows, :] = _residual(x, y, gate_ref, gpost_ref, FFN_RES_WEIGHT)


def _row_specs(tm, d, tiles_per_batch):
    row = pl.BlockSpec((tm, d), lambda i: (i, 0))
    mod = pl.BlockSpec((1, 1, d), lambda i: (i // tiles_per_batch, 0, 0))
    vec = pl.BlockSpec((1, d), lambda i: (0, 0))
    return row, mod, vec


def _stacked_resident(stacked, index):
    lead = len(index)
    shape = (None,) * lead + tuple(stacked.shape[lead:])
    return pl.BlockSpec(shape, lambda *_: tuple(index) + (0,) * (stacked.ndim - lead),
                        pipeline_mode=pl.Buffered(1))


def _ffn(x2, seq, shift, scale, gate, g_pre, g_post, w_in, w_out, index):
    t, d = x2.shape
    tm = min(FFN_ROW_TILE, seq)
    f = w_out.shape[-2]
    assert f % FFN_CHUNK == 0 and tm % FFN_SUBTILE == 0
    row, mod, vec = _row_specs(tm, d, seq // tm)
    return pl.pallas_call(
        _ffn_kernel,
        grid=(t // tm,),
        in_specs=[row, mod, mod, mod, vec, vec, _stacked_resident(w_in, index),
                  _stacked_resident(w_out, index)],
        out_specs=row,
        out_shape=jax.ShapeDtypeStruct((t, d), F32),
        scratch_shapes=[pltpu.VMEM((tm, f), BF16)],
        compiler_params=_params(1),
        name="ffn",
    )(x2, shift, scale, gate, g_pre, g_post, w_in, w_out)


def _causal_conv(u, hist_ref, w_ref):
    tm = u.shape[0]
    kw = w_ref.shape[0]
    hist_ref[V7X_SUBLANES:, :] = u
    out = w_ref[kw - 1:kw] * u
    for k in range(1, kw):
        out = out + w_ref[kw - 1 - k:kw - k] * hist_ref[V7X_SUBLANES - k:V7X_SUBLANES - k + tm, :]
    hist_ref[:V7X_SUBLANES, :] = u[tm - V7X_SUBLANES:]
    return out


def _sconv_kernel(tiles_per_batch, x_ref, shift_ref, scale_ref, gate_ref, gpre_ref, gpost_ref,
                  win_ref, convw_ref, wout_ref, o_ref, hist_ref):
    d = x_ref.shape[1]

    @pl.when(pl.program_id(0) % tiles_per_batch == 0)
    def _():
        hist_ref[:V7X_SUBLANES, :] = jnp.zeros((V7X_SUBLANES, d), F32)

    x = x_ref[...]
    h = _pre(x, gpre_ref, scale_ref, shift_ref).astype(BF16)
    c_gate = jnp.dot(h, win_ref[:, d:2 * d], preferred_element_type=F32)
    xv = jnp.dot(h, win_ref[:, 2 * d:], preferred_element_type=F32)
    u = c_gate * xv
    conv = _causal_conv(u, hist_ref, convw_ref)
    b_gate = jnp.dot(h, win_ref[:, :d], preferred_element_type=F32)
    y = jnp.dot((b_gate * conv).astype(BF16), wout_ref[...], preferred_element_type=F32)
    o_ref[...] = _residual(x, y, gate_ref, gpost_ref)


def _sconv(x2, seq, shift, scale, gate, g_pre, g_post, w_in, conv_w, w_out):
    t, d = x2.shape
    tm = min(ROW_TILE, seq)
    row, mod, vec = _row_specs(tm, d, seq // tm)
    return pl.pallas_call(
        functools.partial(_sconv_kernel, seq // tm),
        grid=(t // tm,),
        in_specs=[row, mod, mod, mod, vec, vec, _resident(w_in.shape), _resident(conv_w.shape),
                  _resident(w_out.shape)],
        out_specs=row,
        out_shape=jax.ShapeDtypeStruct((t, d), F32),
        scratch_shapes=[pltpu.VMEM((V7X_SUBLANES + tm, d), F32)],
        compiler_params=_params(1),
        name="sconv_mixer",
    )(x2, shift, scale, gate, g_pre, g_post, w_in.astype(BF16), conv_w, w_out.astype(BF16))


def _linear_scan(a, b, h_prev):
    n, c = a.shape
    grouped = (n // V7X_SUBLANES, V7X_SUBLANES, c)
    sub = lax.broadcasted_iota(jnp.int32, grouped, 1)
    a = a.reshape(grouped)
    b = b.reshape(grouped)
    d = 1
    while d < V7X_SUBLANES:
        keep = sub >= d
        b = jnp.where(keep, b + a * pltpu.roll(b, d, axis=1), b)
        a = jnp.where(keep, a * pltpu.roll(a, d, axis=1), a)
        d *= 2
    a = a.reshape(n, c)
    b = b.reshape(n, c)
    groups = []
    for r in range(0, n, V7X_SUBLANES):
        h_group = a[r:r + V7X_SUBLANES] * h_prev + b[r:r + V7X_SUBLANES]
        groups.append(h_group)
        h_prev = h_group[V7X_SUBLANES - 1:]
    return jnp.concatenate(groups, axis=0)


def _lru_kernel(tiles_per_batch, x_ref, shift_ref, scale_ref, gate_ref, gpre_ref, gpost_ref,
                win_ref, convw_ref, convb_ref, wgate_ref, ba_ref, bx_ref, lam_ref, wout_ref,
                o_ref, xhist_ref, hcarry_ref):
    tm = x_ref.shape[0]
    w = wout_ref.shape[0]
    n_groups, gk, two_gn = wgate_ref.shape
    gn = two_gn // 2

    @pl.when(pl.program_id(0) % tiles_per_batch == 0)
    def _():
        xhist_ref[:V7X_SUBLANES, :] = jnp.zeros((V7X_SUBLANES, w), F32)
        hcarry_ref[...] = jnp.zeros(hcarry_ref.shape, F32)

    x = x_ref[...]
    h = _pre(x, gpre_ref, scale_ref, shift_ref).astype(BF16)
    xraw = jnp.dot(h, win_ref[:, w:], preferred_element_type=F32)
    xb = _causal_conv(xraw, xhist_ref, convw_ref) + convb_ref[...]

    xb16 = xb.astype(BF16)
    r_parts, i_parts = [], []
    for g in range(n_groups):
        ri = jnp.dot(xb16[:, g * gk:(g + 1) * gk], wgate_ref[g], preferred_element_type=F32)
        r_parts.append(ri[:, :gn])
        i_parts.append(ri[:, gn:])
    r = jax.nn.sigmoid(jnp.concatenate(r_parts, axis=1) + ba_ref[...])
    i = jax.nn.sigmoid(jnp.concatenate(i_parts, axis=1) + bx_ref[...])

    neg_lam = -lam_ref[...]
    softplus = jnp.maximum(neg_lam, 0.0) + jnp.log1p(jnp.exp(-jnp.abs(neg_lam)))
    log_a = (-LRU_C * softplus) * r
    a = jnp.exp(log_a)
    mult = jnp.sqrt(-jnp.tanh(log_a) * (1.0 + a * a))
    hs = _linear_scan(a, mult * (i * xb), hcarry_ref[V7X_SUBLANES - 1:])
    hcarry_ref[...] = hs[tm - V7X_SUBLANES:]

    gate_br = jnp.dot(h, win_ref[:, :w], preferred_element_type=F32)
    y = jnp.dot((hs * jax.nn.gelu(gate_br)).astype(BF16), wout_ref[...], preferred_element_type=F32)
    o_ref[...] = _residual(x, y, gate_ref, gpost_ref)


def _lru_gate_weights(w_a, w_x):
    nb, bd, _ = w_a.shape
    per = V7X_MXU_DIM // bd
    assert nb % per == 0

    def dense(wb):
        wg = wb.reshape(nb // per, per, bd, bd)
        eye = jnp.eye(per, dtype=wb.dtype)
        return jnp.einsum("gpij,pq->gpiqj", wg, eye).reshape(nb // per, per * bd, per * bd)

    return jnp.concatenate([dense(w_a), dense(w_x)], axis=-1).astype(BF16)


def _lru(x2, seq, shift, scale, gate, g_pre, g_post, w_in, conv_w, conv_b, w_a, b_a, w_x, b_x, lam,
         w_out):
    t, d = x2.shape
    w = w_out.shape[0]
    tm = min(ROW_TILE, seq)
    row, mod, vec = _row_specs(tm, d, seq // tm)
    wvec = pl.BlockSpec((1, w), lambda i: (0, 0))
    w_gate = _lru_gate_weights(w_a, w_x)
    return pl.pallas_call(
        functools.partial(_lru_kernel, seq // tm),
        grid=(t // tm,),
        in_specs=[row, mod, mod, mod, vec, vec, _resident(w_in.shape), _resident(conv_w.shape), wvec,
                  _resident(w_gate.shape), wvec, wvec, wvec, _resident(w_out.shape)],
        out_specs=row,
        out_shape=jax.ShapeDtypeStruct((t, d), F32),
        scratch_shapes=[pltpu.VMEM((V7X_SUBLANES + tm, w), F32), pltpu.VMEM((V7X_SUBLANES, w), F32)],
        compiler_params=_params(1),
        name="lru_mixer",
    )(x2, shift, scale, gate, g_pre, g_post, w_in.astype(BF16), conv_w, conv_b.reshape(1, w), w_gate,
      b_a.reshape(1, w), b_x.reshape(1, w), lam.reshape(1, w), w_out.astype(BF16))


def _fox_proj_kernel(tiles_per_batch, n_heads, x_ref, shift_ref, scale_ref, gpre_ref, wqk_ref, wvt_ref, wf_ref,
                     bf_ref, tri_ref, pq_ref, pk_ref, oneq_ref, onek_ref,
                     q_ref, k_ref, vt_ref, augq_ref, augk_ref, carry_ref):
    d = x_ref.shape[1]
    n_pairs = q_ref.shape[1]
    tk = vt_ref.shape[4]

    @pl.when(pl.program_id(0) % tiles_per_batch == 0)
    def _():
        carry_ref[...] = jnp.zeros(carry_ref.shape, F32)

    h = _pre(x_ref[...], gpre_ref, scale_ref, shift_ref).astype(BF16)
    q = jnp.dot(h, wqk_ref[:, :d], preferred_element_type=F32) * (FOX_HEAD_DIM ** -0.5 * LOG2E)
    k = jnp.dot(h, wqk_ref[:, d:], preferred_element_type=F32)
    vt = lax.dot_general(wvt_ref[...], h, (((1,), (1,)), ((), ())), preferred_element_type=F32)
    for p in range(n_pairs):
        sl = slice(p * V7X_LANES, (p + 1) * V7X_LANES)
        q_ref[0, p] = q[:, sl].astype(BF16)
        k_ref[0, p] = k[:, sl].astype(BF16)
        for j in range(vt_ref.shape[2]):
            vt_ref[0, p, j] = vt[sl, j * tk:(j + 1) * tk].astype(BF16)

    f_logit = jnp.dot(h, wf_ref[...], preferred_element_type=F32) + bf_ref[...]
    log_f = jnp.minimum(f_logit, 0.0) - jnp.log1p(jnp.exp(-jnp.abs(f_logit)))
    lanes = lax.broadcasted_iota(jnp.int32, log_f.shape, 1)
    log_f = jnp.where(lanes < n_heads, log_f, 0.0)
    tri = tri_ref[...]
    cum = carry_ref[V7X_SUBLANES - 1:]
    for part in _split3(log_f):
        cum = cum + jnp.dot(tri, part, preferred_element_type=F32)
    carry_ref[...] = cum[cum.shape[0] - V7X_SUBLANES:]

    parts = jnp.concatenate(_split3(cum * LOG2E), axis=1)
    augq_ref[...] = (jnp.dot(parts, pq_ref[...], preferred_element_type=F32) + oneq_ref[...]).astype(BF16)
    augk_ref[...] = (jnp.dot(parts, pk_ref[...], preferred_element_type=F32) + onek_ref[...]).astype(BF16)


def _aug_constants(n_heads):
    assert n_heads * AUG_PER_HEAD <= V7X_LANES
    pq = np.zeros((3 * V7X_LANES, V7X_LANES), np.float32)
    pk = np.zeros((3 * V7X_LANES, V7X_LANES), np.float32)
    oneq = np.zeros((1, V7X_LANES), np.float32)
    onek = np.zeros((1, V7X_LANES), np.float32)
    for h in range(n_heads):
        for j in range(3):
            pq[j * V7X_LANES + h, AUG_PER_HEAD * h + j] = 1.0
            pk[j * V7X_LANES + h, AUG_PER_HEAD * h + 3 + j] = -1.0
            oneq[0, AUG_PER_HEAD * h + 3 + j] = 1.0
            onek[0, AUG_PER_HEAD * h + j] = 1.0
    return (jnp.asarray(pq, BF16), jnp.asarray(pk, BF16), jnp.asarray(oneq), jnp.asarray(onek))


def _fox_proj(x2, bsz, seq, shift, scale, g_pre, w_in, b_f):
    t, d = x2.shape
    n_heads = d // FOX_HEAD_DIM
    n_pairs = d // V7X_LANES
    tm = min(ROW_TILE, seq)
    tiles_per_batch = seq // tm
    row, mod, vec = _row_specs(tm, d, tiles_per_batch)
    tk = min(ATTN_TILE, seq)
    assert tm % tk == 0
    w_qk = w_in[:, :2 * d].astype(BF16)
    w_vt = w_in[:, 2 * d:3 * d].T.astype(BF16)
    w_f = jnp.pad(w_in[:, 3 * d:], ((0, 0), (0, V7X_LANES - n_heads))).astype(BF16)
    b_f = jnp.pad(b_f, (0, V7X_LANES - n_heads)).reshape(1, V7X_LANES)
    tri = jnp.asarray(np.tril(np.ones((tm, tm), np.float32)), BF16)
    pq, pk, oneq, onek = _aug_constants(n_heads)
    pair_spec = pl.BlockSpec((1, n_pairs, tm, V7X_LANES),
                             lambda i: (i // tiles_per_batch, 0, i % tiles_per_batch, 0))
    vt_spec = pl.BlockSpec((1, n_pairs, tm // tk, V7X_LANES, tk),
                           lambda i: (i // tiles_per_batch, 0, i % tiles_per_batch, 0, 0))
    aug_spec = pl.BlockSpec((tm, V7X_LANES), lambda i: (i, 0))
    pair_shape = jax.ShapeDtypeStruct((bsz, n_pairs, seq, V7X_LANES), BF16)
    vt_shape = jax.ShapeDtypeStruct((bsz, n_pairs, seq // tk, V7X_LANES, tk), BF16)
    aug_shape = jax.ShapeDtypeStruct((t, V7X_LANES), BF16)
    return pl.pallas_call(
        functools.partial(_fox_proj_kernel, tiles_per_batch, n_heads),
        grid=(t // tm,),
        in_specs=[row, mod, mod, vec, _resident(w_qk.shape), _resident(w_vt.shape), _resident(w_f.shape),
                  _resident(b_f.shape), _resident(tri.shape), _resident(pq.shape), _resident(pk.shape),
                  _resident(oneq.shape), _resident(onek.shape)],
        out_specs=[pair_spec, pair_spec, vt_spec, aug_spec, aug_spec],
        out_shape=[pair_shape, pair_shape, vt_shape, aug_shape, aug_shape],
        scratch_shapes=[pltpu.VMEM((V7X_SUBLANES, V7X_LANES), F32)],
        compiler_params=_params(1),
        name="fox_proj",
    )(x2, shift, scale, g_pre, w_qk, w_vt, w_f, b_f, tri, pq, pk, oneq, onek)


def _fox_attn_kernel(group_pairs, x_ref, gate_ref, gpost_ref, q_ref, k_ref, vt_ref, augq_ref, augk_ref,
                     wout_ref, o_ref, lhs_ref, stage_ref, mx_ref, alpha_ref, acc_ref, heads_ref):
    tq = x_ref.shape[0]
    n_pairs = q_ref.shape[1]
    qi = pl.program_id(1)
    lanes = lax.broadcasted_iota(jnp.int32, (tq, V7X_LANES), 1)
    half = lanes // FOX_HEAD_DIM
    causal_t = (lax.broadcasted_iota(jnp.int32, (tq, tq), 0)
                <= lax.broadcasted_iota(jnp.int32, (tq, tq), 1))
    augq = augq_ref[...]
    contract_last = (((1,), (1,)), ((), ()))
    zero16 = jnp.zeros((), BF16)
    ones_rows = jnp.ones((FOX_HEAD_DIM, tq), BF16)
    n_slabs = tq // V7X_SUBLANES

    def key_rows(kb):
        return pl.ds(pl.multiple_of(kb * tq, tq), tq)

    def over_rows(vec, n_rows):
        return jnp.concatenate([vec] * (n_rows // V7X_SUBLANES), axis=0)

    def stage(pairs):
        for p in pairs:
            q_pair = q_ref[0, p]
            for e in range(2):
                h = 2 * p + e
                own_aug = (lanes >= AUG_PER_HEAD * h) & (lanes < AUG_PER_HEAD * (h + 1))
                lhs_ref[h] = jnp.concatenate(
                    [jnp.where(half == e, q_pair, zero16), jnp.where(own_aug, augq, zero16)], axis=1)
                mx_ref[h] = jnp.full((V7X_SUBLANES, tq), -jnp.inf, F32)
                acc_ref[h] = jnp.zeros((V7X_LANES, tq), F32)

    def attend(pairs, kb, n_blocks, masked, slot):
        ones = jnp.concatenate([ones_rows] * n_blocks, axis=1)
        for p in pairs:
            keys = [jnp.concatenate([k_ref[0, p, key_rows(kb + j), :], augk_ref[key_rows(kb + j), :]], axis=1)
                    for j in range(n_blocks)]
            for e in range(2):
                h = 2 * p + e
                hh = h - 2 * pairs[0]
                slabs = []
                for j in range(n_blocks):
                    s = lax.dot_general(keys[j], lhs_ref[h], contract_last, preferred_element_type=F32)
                    if masked:
                        s = jnp.where(causal_t, s, -jnp.inf)
                    stage_ref[slot, hh, j] = s
                    slabs += [s[i * V7X_SUBLANES:(i + 1) * V7X_SUBLANES] for i in range(n_slabs)]
                while len(slabs) > 1:
                    slabs = [jnp.maximum(a, b) for a, b in zip(slabs[0::2], slabs[1::2])]
                m_old = mx_ref[h]
                m_new = jnp.maximum(m_old, jnp.broadcast_to(jnp.max(slabs[0], axis=0, keepdims=True),
                                                            (V7X_SUBLANES, tq)))
                mx_ref[h] = m_new
                alpha_ref[h] = jnp.exp2(m_old - m_new)
        for p in pairs:
            vt_pair = jnp.concatenate([vt_ref[0, p, kb + j] for j in range(n_blocks)], axis=1)
            for e in range(2):
                h = 2 * p + e
                hh = h - 2 * pairs[0]
                m_rows = over_rows(mx_ref[h], tq)
                prob = jnp.concatenate(
                    [jnp.exp2(stage_ref[slot, hh, j] - m_rows).astype(BF16) for j in range(n_blocks)], axis=0)
                vt_own = (jnp.concatenate([vt_pair[:FOX_HEAD_DIM], ones], axis=0) if e == 0 else
                          jnp.concatenate([ones, vt_pair[FOX_HEAD_DIM:]], axis=0))
                acc_ref[h] = (over_rows(alpha_ref[h], V7X_LANES) * acc_ref[h]
                              + jnp.dot(vt_own, prob, preferred_element_type=F32))

    def finalize(pairs):
        for p in pairs:
            a_even = acc_ref[2 * p]
            a_odd = acc_ref[2 * p + 1]
            numer = jnp.concatenate([a_even[:FOX_HEAD_DIM], a_odd[FOX_HEAD_DIM:]], axis=0)
            denom = jnp.concatenate([a_even[FOX_HEAD_DIM:], a_odd[:FOX_HEAD_DIM]], axis=0)
            heads_ref[p] = (numer / denom).T.astype(BF16)

    for g in range(n_pairs // group_pairs):
        pairs = range(g * group_pairs, (g + 1) * group_pairs)
        stage(pairs)
        attend(pairs, qi, 1, True, qi % 2)

        def body(i, c, pairs=pairs):
            attend(pairs, 2 * i, 2, False, i % 2)
            return c

        lax.fori_loop(0, qi // 2, body, 0)

        @pl.when(qi % 2 == 1)
        def _(pairs=pairs):
            attend(pairs, qi - 1, 1, False, qi % 2)

        finalize(pairs)

    o = jnp.concatenate([heads_ref[p] for p in range(n_pairs)], axis=1)
    y = jnp.dot(o, wout_ref[...], preferred_element_type=F32)
    o_ref[...] = _residual(x_ref[...], y, gate_ref, gpost_ref)


def _fox_attn(x2, bsz, seq, gate, g_post, q, k, vt, augq, augk, w_out):
    t, d = x2.shape
    n_pairs = d // V7X_LANES
    tq = min(ATTN_TILE, seq)
    nq = seq // tq
    group_pairs = min(ATTN_GROUP_PAIRS, n_pairs)
    assert n_pairs % group_pairs == 0
    row = pl.BlockSpec((tq, d), lambda b, i: (b * nq + i, 0))
    mod = pl.BlockSpec((1, 1, d), lambda b, i: (b, 0, 0))
    vec = pl.BlockSpec((1, d), lambda b, i: (0, 0))
    q_spec = pl.BlockSpec((1, n_pairs, tq, V7X_LANES), lambda b, i: (b, 0, i, 0))
    k_spec = pl.BlockSpec((1, n_pairs, seq, V7X_LANES), lambda b, i: (b, 0, 0, 0))
    vt_spec = pl.BlockSpec((1, n_pairs, nq, V7X_LANES, tq), lambda b, i: (b, 0, 0, 0, 0))
    augq_spec = pl.BlockSpec((tq, V7X_LANES), lambda b, i: (b * nq + i, 0))
    augk_spec = pl.BlockSpec((seq, V7X_LANES), lambda b, i: (b, 0))
    return pl.pallas_call(
        functools.partial(_fox_attn_kernel, group_pairs),
        grid=(bsz, nq),
        in_specs=[row, mod, vec, q_spec, k_spec, vt_spec, augq_spec, augk_spec, _resident(w_out.shape)],
        out_specs=row,
        out_shape=jax.ShapeDtypeStruct((t, d), F32),
        scratch_shapes=[pltpu.VMEM((2 * n_pairs, tq, 2 * V7X_LANES), BF16),
                        pltpu.VMEM((2, 2 * group_pairs, 2, tq, tq), F32),
                        pltpu.VMEM((2 * n_pairs, V7X_SUBLANES, tq), F32),
                        pltpu.VMEM((2 * n_pairs, V7X_SUBLANES, tq), F32),
                        pltpu.VMEM((2 * n_pairs, V7X_LANES, tq), F32),
                        pltpu.VMEM((n_pairs, tq, V7X_LANES), BF16)],
        compiler_params=_params(2),
        name="fox_attn",
    )(x2, gate, g_post, q, k, vt, augq, augk, w_out.astype(BF16))


def kernel(x, c, w_cond, b_cond, norm_pre, norm_post, w_ffn_in, w_ffn_out, fox_w_in, fox_b_f, fox_w_out, sconv_w_in, sconv_conv_w, sconv_w_out, lru_w_in, lru_conv_w, lru_conv_b, lru_w_a, lru_b_a, lru_w_x, lru_b_x, lru_lambda, lru_w_out):
    bsz, seq, d = x.shape
    depth = w_cond.shape[0]
    assert d % V7X_LANES == 0 and V7X_LANES == 2 * FOX_HEAD_DIM
    assert seq % min(ROW_TILE, seq) == 0 and seq % min(ATTN_TILE, seq) == 0

    mod = _modulation(c, w_cond, b_cond).reshape(depth, bsz, N_SUB, 3, 1, d)
    x2 = x.reshape(bsz * seq, d)
    w_ffn_in16 = w_ffn_in.astype(BF16)
    w_ffn_out16 = w_ffn_out.astype(BF16)

    for i in range(depth):
        def mods(s):
            return mod[i, :, s, 0], mod[i, :, s, 1], mod[i, :, s, 2]

        def gains(s):
            return norm_pre[i, s].reshape(1, d), norm_post[i, s].reshape(1, d)

        shift, scale, gate = mods(0)
        x2 = _ffn(x2, seq, shift, scale, gate, *gains(0), w_ffn_in16, w_ffn_out16, (i, 0))

        shift, scale, gate = mods(1)
        g_pre, g_post = gains(1)
        kind, j = i % N_MIXERS, i // N_MIXERS
        if kind == 0:
            q, k, vt, augq, augk = _fox_proj(x2, bsz, seq, shift, scale, g_pre, fox_w_in[j], fox_b_f[j])
            x2 = _fox_attn(x2, bsz, seq, gate, g_post, q, k, vt, augq, augk, fox_w_out[j])
        elif kind == 1:
            x2 = _sconv(x2, seq, shift, scale, gate, g_pre, g_post, sconv_w_in[j], sconv_conv_w[j],
                        sconv_w_out[j])
        else:
            x2 = _lru(x2, seq, shift, scale, gate, g_pre, g_post, lru_w_in[j], lru_conv_w[j],
                      lru_conv_b[j], lru_w_a[j], lru_b_a[j], lru_w_x[j], lru_b_x[j], lru_lambda[j],
                      lru_w_out[j])

        shift, scale, gate = mods(2)
        x2 = _ffn(x2, seq, shift, scale, gate, *gains(2), w_ffn_in16, w_ffn_out16, (i, 1))

    return x2.reshape(bsz, seq, d)
```

```python
import functools

import numpy as np
import jax
import jax.numpy as jnp
from jax import lax
from jax.experimental import pallas as pl
from jax.experimental.pallas import tpu as pltpu

N_SUB = 3
N_MIXERS = 3
FFN_RES_WEIGHT = 0.5
RMS_EPS = 1e-6
FOX_HEAD_DIM = 64
LRU_C = 8.0
LOG2E = 1.4426950408889634

V7X_LANES = 128
V7X_SUBLANES = 8
V7X_MXU_DIM = 256
V7X_VMEM_BYTES = 64 * 1024 * 1024
VMEM_LIMIT_BYTES = V7X_VMEM_BYTES - 8 * 1024 * 1024

ROW_TILE = 512
ATTN_TILE = 256
FFN_CHUNK = V7X_MXU_DIM
FFN_ROW_TILE = 512
FFN_SUBTILE = 256
AUG_PER_HEAD = 6
ATTN_GROUP_PAIRS = 8
MOD_COL_TILE = 1536

BF16 = jnp.bfloat16
F32 = jnp.float32


def _params(n_axes):
    return pltpu.CompilerParams(dimension_semantics=("arbitrary",) * n_axes,
                                vmem_limit_bytes=VMEM_LIMIT_BYTES)


def _resident(shape):
    return pl.BlockSpec(shape, lambda *_: (0,) * len(shape), pipeline_mode=pl.Buffered(1))


def _rms(x):
    return x * lax.rsqrt(jnp.mean(x * x, axis=-1, keepdims=True) + RMS_EPS)


def _pre(x, gpre_ref, scale_ref, shift_ref):
    return _rms(x) * (gpre_ref[...] * (1.0 + scale_ref[0])) + shift_ref[0]


def _residual(x, y, gate_ref, gpost_ref, weight=1.0):
    return x + _rms(y) * ((weight * gate_ref[0]) * gpost_ref[...])


def _split3(x):
    hi = x.astype(BF16)
    r1 = x - hi.astype(F32)
    mid = r1.astype(BF16)
    lo = (r1 - mid.astype(F32)).astype(BF16)
    return hi, mid, lo


def _mod_kernel(c_ref, w_ref, b_ref, o_ref):
    c = c_ref[...]
    c_act = (c * jax.nn.sigmoid(c)).astype(BF16)
    o_ref[0] = jnp.dot(c_act, w_ref[0].astype(BF16), preferred_element_type=F32) + b_ref[0]


def _modulation(c, w_cond, b_cond):
    depth, d, n = w_cond.shape
    bsz = c.shape[0]
    tn = MOD_COL_TILE if n % MOD_COL_TILE == 0 else n
    return pl.pallas_call(
        _mod_kernel,
        grid=(depth, n // tn),
        in_specs=[
            pl.BlockSpec((bsz, d), lambda i, j: (0, 0)),
            pl.BlockSpec((1, d, tn), lambda i, j: (i, 0, j)),
            pl.BlockSpec((1, 1, tn), lambda i, j: (i, 0, j)),
        ],
        out_specs=pl.BlockSpec((1, bsz, tn), lambda i, j: (i, 0, j)),
        out_shape=jax.ShapeDtypeStruct((depth, bsz, n), F32),
        compiler_params=_params(2),
        name="adaln_mod",
    )(c, w_cond, b_cond.reshape(depth, 1, n))


def _ffn_kernel(x_ref, shift_ref, scale_ref, gate_ref, gpre_ref, gpost_ref, win_ref, wout_ref,
                o_ref, act_ref):
    f = wout_ref.shape[0]
    fc = FFN_CHUNK
    tm = x_ref.shape[0]
    for r in range(0, tm, FFN_SUBTILE):
        rows = slice(r, r + FFN_SUBTILE)
        x = x_ref[rows, :]
        h = _pre(x, gpre_ref, scale_ref, shift_ref)
        for c in range(f // fc):
            g = jnp.dot(h, win_ref[:, c * fc:(c + 1) * fc], preferred_element_type=F32)
            u = jnp.dot(h, win_ref[:, f + c * fc:f + (c + 1) * fc], preferred_element_type=F32)
            act_ref[rows, c * fc:(c + 1) * fc] = (g * jax.nn.sigmoid(g) * u).astype(BF16)
        y = jnp.dot(act_ref[rows, :].astype(F32), wout_ref[...], preferred_element_type=F32)
        o_ref[rows, :] = _residual(x, y, gate_ref, gpost_ref, FFN_RES_WEIGHT)


def _row_specs(tm, d, tiles_per_batch):
    row = pl.BlockSpec((tm, d), lambda i: (i, 0))
    mod = pl.BlockSpec((1, 1, d), lambda i: (i // tiles_per_batch, 0, 0))
    vec = pl.BlockSpec((1, d), lambda i: (0, 0))
    return row, mod, vec


def _stacked_resident(stacked, index):
    lead = len(index)
    shape = (None,) * lead + tuple(stacked.shape[lead:])
    return pl.BlockSpec(shape, lambda *_: tuple(index) + (0,) * (stacked.ndim - lead),
                        pipeline_mode=pl.Buffered(1))


def _ffn(x2, seq, shift, scale, gate, g_pre, g_post, w_in, w_out, index):
    t, d = x2.shape
    tm = min(FFN_ROW_TILE, seq)
    f = w_out.shape[-2]
    assert f % FFN_CHUNK == 0 and tm % FFN_SUBTILE == 0
    row, mod, vec = _row_specs(tm, d, seq // tm)
    return pl.pallas_call(
        _ffn_kernel,
        grid=(t // tm,),
        in_specs=[row, mod, mod, mod, vec, vec, _stacked_resident(w_in, index),
                  _stacked_resident(w_out, index)],
        out_specs=row,
        out_shape=jax.ShapeDtypeStruct((t, d), F32),
        scratch_shapes=[pltpu.VMEM((tm, f), BF16)],
        compiler_params=_params(1),
        name="ffn",
    )(x2, shift, scale, gate, g_pre, g_post, w_in, w_out)


def _causal_conv(u, hist_ref, w_ref):
    tm = u.shape[0]
    kw = w_ref.shape[0]
    hist_ref[V7X_SUBLANES:, :] = u
    out = w_ref[kw - 1:kw] * u
    for k in range(1, kw):
        out = out + w_ref[kw - 1 - k:kw - k] * hist_ref[V7X_SUBLANES - k:V7X_SUBLANES - k + tm, :]
    hist_ref[:V7X_SUBLANES, :] = u[tm - V7X_SUBLANES:]
    return out


def _sconv_kernel(tiles_per_batch, x_ref, shift_ref, scale_ref, gate_ref, gpre_ref, gpost_ref,
                  win_ref, convw_ref, wout_ref, o_ref, hist_ref):
    d = x_ref.shape[1]

    @pl.when(pl.program_id(0) % tiles_per_batch == 0)
    def _():
        hist_ref[:V7X_SUBLANES, :] = jnp.zeros((V7X_SUBLANES, d), F32)

    x = x_ref[...]
    h = _pre(x, gpre_ref, scale_ref, shift_ref).astype(BF16)
    c_gate = jnp.dot(h, win_ref[:, d:2 * d], preferred_element_type=F32)
    xv = jnp.dot(h, win_ref[:, 2 * d:], preferred_element_type=F32)
    u = c_gate * xv
    conv = _causal_conv(u, hist_ref, convw_ref)
    b_gate = jnp.dot(h, win_ref[:, :d], preferred_element_type=F32)
    y = jnp.dot((b_gate * conv).astype(BF16), wout_ref[...], preferred_element_type=F32)
    o_ref[...] = _residual(x, y, gate_ref, gpost_ref)


def _sconv(x2, seq, shift, scale, gate, g_pre, g_post, w_in, conv_w, w_out):
    t, d = x2.shape
    tm = min(ROW_TILE, seq)
    row, mod, vec = _row_specs(tm, d, seq // tm)
    return pl.pallas_call(
        functools.partial(_sconv_kernel, seq // tm),
        grid=(t // tm,),
        in_specs=[row, mod, mod, mod, vec, vec, _resident(w_in.shape), _resident(conv_w.shape),
                  _resident(w_out.shape)],
        out_specs=row,
        out_shape=jax.ShapeDtypeStruct((t, d), F32),
        scratch_shapes=[pltpu.VMEM((V7X_SUBLANES + tm, d), F32)],
        compiler_params=_params(1),
        name="sconv_mixer",
    )(x2, shift, scale, gate, g_pre, g_post, w_in.astype(BF16), conv_w, w_out.astype(BF16))


def _linear_scan(a, b, h_prev):
    n, c = a.shape
    grouped = (n // V7X_SUBLANES, V7X_SUBLANES, c)
    sub = lax.broadcasted_iota(jnp.int32, grouped, 1)
    a = a.reshape(grouped)
    b = b.reshape(grouped)
    d = 1
    while d < V7X_SUBLANES:
        keep = sub >= d
        b = jnp.where(keep, b + a * pltpu.roll(b, d, axis=1), b)
        a = jnp.where(keep, a * pltpu.roll(a, d, axis=1), a)
        d *= 2
    a = a.reshape(n, c)
    b = b.reshape(n, c)
    groups = []
    for r in range(0, n, V7X_SUBLANES):
        h_group = a[r:r + V7X_SUBLANES] * h_prev + b[r:r + V7X_SUBLANES]
        groups.append(h_group)
        h_prev = h_group[V7X_SUBLANES - 1:]
    return jnp.concatenate(groups, axis=0)


def _lru_kernel(tiles_per_batch, x_ref, shift_ref, scale_ref, gate_ref, gpre_ref, gpost_ref,
                win_ref, convw_ref, convb_ref, wgate_ref, ba_ref, bx_ref, lam_ref, wout_ref,
                o_ref, xhist_ref, hcarry_ref):
    tm = x_ref.shape[0]
    w = wout_ref.shape[0]
    n_groups, gk, two_gn = wgate_ref.shape
    gn = two_gn // 2

    @pl.when(pl.program_id(0) % tiles_per_batch == 0)
    def _():
        xhist_ref[:V7X_SUBLANES, :] = jnp.zeros((V7X_SUBLANES, w), F32)
        hcarry_ref[...] = jnp.zeros(hcarry_ref.shape, F32)

    x = x_ref[...]
    h = _pre(x, gpre_ref, scale_ref, shift_ref).astype(BF16)
    xraw = jnp.dot(h, win_ref[:, w:], preferred_element_type=F32)
    xb = _causal_conv(xraw, xhist_ref, convw_ref) + convb_ref[...]

    xb16 = xb.astype(BF16)
    r_parts, i_parts = [], []
    for g in range(n_groups):
        ri = jnp.dot(xb16[:, g * gk:(g + 1) * gk], wgate_ref[g], preferred_element_type=F32)
        r_parts.append(ri[:, :gn])
        i_parts.append(ri[:, gn:])
    r = jax.nn.sigmoid(jnp.concatenate(r_parts, axis=1) + ba_ref[...])
    i = jax.nn.sigmoid(jnp.concatenate(i_parts, axis=1) + bx_ref[...])

    neg_lam = -lam_ref[...]
    softplus = jnp.maximum(neg_lam, 0.0) + jnp.log1p(jnp.exp(-jnp.abs(neg_lam)))
    log_a = (-LRU_C * softplus) * r
    a = jnp.exp(log_a)
    mult = jnp.sqrt(-jnp.tanh(log_a) * (1.0 + a * a))
    hs = _linear_scan(a, mult * (i * xb), hcarry_ref[V7X_SUBLANES - 1:])
    hcarry_ref[...] = hs[tm - V7X_SUBLANES:]

    gate_br = jnp.dot(h, win_ref[:, :w], preferred_element_type=F32)
    y = jnp.dot((hs * jax.nn.gelu(gate_br)).astype(BF16), wout_ref[...], preferred_element_type=F32)
    o_ref[...] = _residual(x, y, gate_ref, gpost_ref)


def _lru_gate_weights(w_a, w_x):
    nb, bd, _ = w_a.shape
    per = V7X_MXU_DIM // bd
    assert nb % per == 0

    def dense(wb):
        wg = wb.reshape(nb // per, per, bd, bd)
        eye = jnp.eye(per, dtype=wb.dtype)
        return jnp.einsum("gpij,pq->gpiqj", wg, eye).reshape(nb // per, per * bd, per * bd)

    return jnp.concatenate([dense(w_a), dense(w_x)], axis=-1).astype(BF16)


def _lru(x2, seq, shift, scale, gate, g_pre, g_post, w_in, conv_w, conv_b, w_a, b_a, w_x, b_x, lam,
         w_out):
    t, d = x2.shape
    w = w_out.shape[0]
    tm = min(ROW_TILE, seq)
    row, mod, vec = _row_specs(tm, d, seq // tm)
    wvec = pl.BlockSpec((1, w), lambda i: (0, 0))
    w_gate = _lru_gate_weights(w_a, w_x)
    return pl.pallas_call(
        functools.partial(_lru_kernel, seq // tm),
        grid=(t // tm,),
        in_specs=[row, mod, mod, mod, vec, vec, _resident(w_in.shape), _resident(conv_w.shape), wvec,
                  _resident(w_gate.shape), wvec, wvec, wvec, _resident(w_out.shape)],
        out_specs=row,
        out_shape=jax.ShapeDtypeStruct((t, d), F32),
        scratch_shapes=[pltpu.VMEM((V7X_SUBLANES + tm, w), F32), pltpu.VMEM((V7X_SUBLANES, w), F32)],
        compiler_params=_params(1),
        name="lru_mixer",
    )(x2, shift, scale, gate, g_pre, g_post, w_in.astype(BF16), conv_w, conv_b.reshape(1, w), w_gate,
      b_a.reshape(1, w), b_x.reshape(1, w), lam.reshape(1, w), w_out.astype(BF16))


def _fox_proj_kernel(tiles_per_batch, n_heads, x_ref, shift_ref, scale_ref, gpre_ref, wqk_ref, wvt_ref, wf_ref,
                     bf_ref, tri_ref, pq_ref, pk_ref, oneq_ref, onek_ref,
                     q_ref, k_ref, vt_ref, augq_ref, augk_ref, carry_ref):
    d = x_ref.shape[1]
    n_pairs = q_ref.shape[1]
    tk = vt_ref.shape[4]

    @pl.when(pl.program_id(0) % tiles_per_batch == 0)
    def _():
        carry_ref[...] = jnp.zeros(carry_ref.shape, F32)

    h = _pre(x_ref[...], gpre_ref, scale_ref, shift_ref).astype(BF16)
    q = jnp.dot(h, wqk_ref[:, :d], preferred_element_type=F32) * (FOX_HEAD_DIM ** -0.5 * LOG2E)
    k = jnp.dot(h, wqk_ref[:, d:], preferred_element_type=F32)
    vt = lax.dot_general(wvt_ref[...], h, (((1,), (1,)), ((), ())), preferred_element_type=F32)
    for p in range(n_pairs):
        sl = slice(p * V7X_LANES, (p + 1) * V7X_LANES)
        q_ref[0, p] = q[:, sl].astype(BF16)
        k_ref[0, p] = k[:, sl].astype(BF16)
        for j in range(vt_ref.shape[2]):
            vt_ref[0, p, j] = vt[sl, j * tk:(j + 1) * tk].astype(BF16)

    f_logit = jnp.dot(h, wf_ref[...], preferred_element_type=F32) + bf_ref[...]
    log_f = jnp.minimum(f_logit, 0.0) - jnp.log1p(jnp.exp(-jnp.abs(f_logit)))
    lanes = lax.broadcasted_iota(jnp.int32, log_f.shape, 1)
    log_f = jnp.where(lanes < n_heads, log_f, 0.0)
    tri = tri_ref[...]
    cum = carry_ref[V7X_SUBLANES - 1:]
    for part in _split3(log_f):
        cum = cum + jnp.dot(tri, part, preferred_element_type=F32)
    carry_ref[...] = cum[cum.shape[0] - V7X_SUBLANES:]

    parts = jnp.concatenate(_split3(cum * LOG2E), axis=1)
    augq_ref[...] = (jnp.dot(parts, pq_ref[...], preferred_element_type=F32) + oneq_ref[...]).astype(BF16)
    augk_ref[...] = (jnp.dot(parts, pk_ref[...], preferred_element_type=F32) + onek_ref[...]).astype(BF16)


def _aug_constants(n_heads):
    assert n_heads * AUG_PER_HEAD <= V7X_LANES
    pq = np.zeros((3 * V7X_LANES, V7X_LANES), np.float32)
    pk = np.zeros((3 * V7X_LANES, V7X_LANES), np.float32)
    oneq = np.zeros((1, V7X_LANES), np.float32)
    onek = np.zeros((1, V7X_LANES), np.float32)
    for h in range(n_heads):
        for j in range(3):
            pq[j * V7X_LANES + h, AUG_PER_HEAD * h + j] = 1.0
            pk[j * V7X_LANES + h, AUG_PER_HEAD * h + 3 + j] = -1.0
            oneq[0, AUG_PER_HEAD * h + 3 + j] = 1.0
            onek[0, AUG_PER_HEAD * h + j] = 1.0
    return (jnp.asarray(pq, BF16), jnp.asarray(pk, BF16), jnp.asarray(oneq), jnp.asarray(onek))


def _fox_proj(x2, bsz, seq, shift, scale, g_pre, w_in, b_f):
    t, d = x2.shape
    n_heads = d // FOX_HEAD_DIM
    n_pairs = d // V7X_LANES
    tm = min(ROW_TILE, seq)
    tiles_per_batch = seq // tm
    row, mod, vec = _row_specs(tm, d, tiles_per_batch)
    tk = min(ATTN_TILE, seq)
    assert tm % tk == 0
    w_qk = w_in[:, :2 * d].astype(BF16)
    w_vt = w_in[:, 2 * d:3 * d].T.astype(BF16)
    w_f = jnp.pad(w_in[:, 3 * d:], ((0, 0), (0, V7X_LANES - n_heads))).astype(BF16)
    b_f = jnp.pad(b_f, (0, V7X_LANES - n_heads)).reshape(1, V7X_LANES)
    tri = jnp.asarray(np.tril(np.ones((tm, tm), np.float32)), BF16)
    pq, pk, oneq, onek = _aug_constants(n_heads)
    pair_spec = pl.BlockSpec((1, n_pairs, tm, V7X_LANES),
                             lambda i: (i // tiles_per_batch, 0, i % tiles_per_batch, 0))
    vt_spec = pl.BlockSpec((1, n_pairs, tm // tk, V7X_LANES, tk),
                           lambda i: (i // tiles_per_batch, 0, i % tiles_per_batch, 0, 0))
    aug_spec = pl.BlockSpec((tm, V7X_LANES), lambda i: (i, 0))
    pair_shape = jax.ShapeDtypeStruct((bsz, n_pairs, seq, V7X_LANES), BF16)
    vt_shape = jax.ShapeDtypeStruct((bsz, n_pairs, seq // tk, V7X_LANES, tk), BF16)
    aug_shape = jax.ShapeDtypeStruct((t, V7X_LANES), BF16)
    return pl.pallas_call(
        functools.partial(_fox_proj_kernel, tiles_per_batch, n_heads),
        grid=(t // tm,),
        in_specs=[row, mod, mod, vec, _resident(w_qk.shape), _resident(w_vt.shape), _resident(w_f.shape),
                  _resident(b_f.shape), _resident(tri.shape), _resident(pq.shape), _resident(pk.shape),
                  _resident(oneq.shape), _resident(onek.shape)],
        out_specs=[pair_spec, pair_spec, vt_spec, aug_spec, aug_spec],
        out_shape=[pair_shape, pair_shape, vt_shape, aug_shape, aug_shape],
        scratch_shapes=[pltpu.VMEM((V7X_SUBLANES, V7X_LANES), F32)],
        compiler_params=_params(1),
        name="fox_proj",
    )(x2, shift, scale, g_pre, w_qk, w_vt, w_f, b_f, tri, pq, pk, oneq, onek)


def _fox_attn_kernel(group_pairs, x_ref, gate_ref, gpost_ref, q_ref, k_ref, vt_ref, augq_ref, augk_ref,
                     wout_ref, o_ref, lhs_ref, stage_ref, mx_ref, alpha_ref, acc_ref, heads_ref):
    tq = x_ref.shape[0]
    n_pairs = q_ref.shape[1]
    qi = pl.program_id(1)
    lanes = lax.broadcasted_iota(jnp.int32, (tq, V7X_LANES), 1)
    half = lanes // FOX_HEAD_DIM
    causal_t = (lax.broadcasted_iota(jnp.int32, (tq, tq), 0)
                <= lax.broadcasted_iota(jnp.int32, (tq, tq), 1))
    augq = augq_ref[...]
    contract_last = (((1,), (1,)), ((), ()))
    zero16 = jnp.zeros((), BF16)
    ones_rows = jnp.ones((FOX_HEAD_DIM, tq), BF16)
    n_slabs = tq // V7X_SUBLANES

    def key_rows(kb):
        return pl.ds(pl.multiple_of(kb * tq, tq), tq)

    def over_rows(vec, n_rows):
        return jnp.concatenate([vec] * (n_rows // V7X_SUBLANES), axis=0)

    def stage(pairs):
        for p in pairs:
            q_pair = q_ref[0, p]
            for e in range(2):
                h = 2 * p + e
                own_aug = (lanes >= AUG_PER_HEAD * h) & (lanes < AUG_PER_HEAD * (h + 1))
                lhs_ref[h] = jnp.concatenate(
                    [jnp.where(half == e, q_pair, zero16), jnp.where(own_aug, augq, zero16)], axis=1)
                mx_ref[h] = jnp.full((V7X_SUBLANES, tq), -jnp.inf, F32)
                acc_ref[h] = jnp.zeros((V7X_LANES, tq), F32)

    def attend(pairs, kb, n_blocks, masked, slot):
        ones = jnp.concatenate([ones_rows] * n_blocks, axis=1)
        for p in pairs:
            keys = [jnp.concatenate([k_ref[0, p, key_rows(kb + j), :], augk_ref[key_rows(kb + j), :]], axis=1)
                    for j in range(n_blocks)]
            for e in range(2):
                h = 2 * p + e
                hh = h - 2 * pairs[0]
                slabs = []
                for j in range(n_blocks):
                    s = lax.dot_general(keys[j], lhs_ref[h], contract_last, preferred_element_type=F32)
                    if masked:
                        s = jnp.where(causal_t, s, -jnp.inf)
                    stage_ref[slot, hh, j] = s
                    slabs += [s[i * V7X_SUBLANES:(i + 1) * V7X_SUBLANES] for i in range(n_slabs)]
                while len(slabs) > 1:
                    slabs = [jnp.maximum(a, b) for a, b in zip(slabs[0::2], slabs[1::2])]
                m_old = mx_ref[h]
                m_new = jnp.maximum(m_old, jnp.broadcast_to(jnp.max(slabs[0], axis=0, keepdims=True),
                                                            (V7X_SUBLANES, tq)))
                mx_ref[h] = m_new
                alpha_ref[h] = jnp.exp2(m_old - m_new)
        for p in pairs:
            vt_pair = jnp.concatenate([vt_ref[0, p, kb + j] for j in range(n_blocks)], axis=1)
            for e in range(2):
                h = 2 * p + e
                hh = h - 2 * pairs[0]
                m_rows = over_rows(mx_ref[h], tq)
                prob = jnp.concatenate(
                    [jnp.exp2(stage_ref[slot, hh, j] - m_rows).astype(BF16) for j in range(n_blocks)], axis=0)
                vt_own = (jnp.concatenate([vt_pair[:FOX_HEAD_DIM], ones], axis=0) if e == 0 else
                          jnp.concatenate([ones, vt_pair[FOX_HEAD_DIM:]], axis=0))
                acc_ref[h] = (over_rows(alpha_ref[h], V7X_LANES) * acc_ref[h]
                              + jnp.dot(vt_own, prob, preferred_element_type=F32))

    def finalize(pairs):
        for p in pairs:
            a_even = acc_ref[2 * p]
            a_odd = acc_ref[2 * p + 1]
            numer = jnp.concatenate([a_even[:FOX_HEAD_DIM], a_odd[FOX_HEAD_DIM:]], axis=0)
            denom = jnp.concatenate([a_even[FOX_HEAD_DIM:], a_odd[:FOX_HEAD_DIM]], axis=0)
            heads_ref[p] = (numer / denom).T.astype(BF16)

    for g in range(n_pairs // group_pairs):
        pairs = range(g * group_pairs, (g + 1) * group_pairs)
        stage(pairs)
        attend(pairs, qi, 1, True, qi % 2)

        def body(i, c, pairs=pairs):
            attend(pairs, 2 * i, 2, False, i % 2)
            return c

        lax.fori_loop(0, qi // 2, body, 0)

        @pl.when(qi % 2 == 1)
        def _(pairs=pairs):
            attend(pairs, qi - 1, 1, False, qi % 2)

        finalize(pairs)

    o = jnp.concatenate([heads_ref[p] for p in range(n_pairs)], axis=1)
    y = jnp.dot(o, wout_ref[...], preferred_element_type=F32)
    o_ref[...] = _residual(x_ref[...], y, gate_ref, gpost_ref)


def _fox_attn(x2, bsz, seq, gate, g_post, q, k, vt, augq, augk, w_out):
    t, d = x2.shape
    n_pairs = d // V7X_LANES
    tq = min(ATTN_TILE, seq)
    nq = seq // tq
    group_pairs = min(ATTN_GROUP_PAIRS, n_pairs)
    assert n_pairs % group_pairs == 0
    row = pl.BlockSpec((tq, d), lambda b, i: (b * nq + i, 0))
    mod = pl.BlockSpec((1, 1, d), lambda b, i: (b, 0, 0))
    vec = pl.BlockSpec((1, d), lambda b, i: (0, 0))
    q_spec = pl.BlockSpec((1, n_pairs, tq, V7X_LANES), lambda b, i: (b, 0, i, 0))
    k_spec = pl.BlockSpec((1, n_pairs, seq, V7X_LANES), lambda b, i: (b, 0, 0, 0))
    vt_spec = pl.BlockSpec((1, n_pairs, nq, V7X_LANES, tq), lambda b, i: (b, 0, 0, 0, 0))
    augq_spec = pl.BlockSpec((tq, V7X_LANES), lambda b, i: (b * nq + i, 0))
    augk_spec = pl.BlockSpec((seq, V7X_LANES), lambda b, i: (b, 0))
    return pl.pallas_call(
        functools.partial(_fox_attn_kernel, group_pairs),
        grid=(bsz, nq),
        in_specs=[row, mod, vec, q_spec, k_spec, vt_spec, augq_spec, augk_spec, _resident(w_out.shape)],
        out_specs=row,
        out_shape=jax.ShapeDtypeStruct((t, d), F32),
        scratch_shapes=[pltpu.VMEM((2 * n_pairs, tq, 2 * V7X_LANES), BF16),
                        pltpu.VMEM((2, 2 * group_pairs, 2, tq, tq), F32),
                        pltpu.VMEM((2 * n_pairs, V7X_SUBLANES, tq), F32),
                        pltpu.VMEM((2 * n_pairs, V7X_SUBLANES, tq), F32),
                        pltpu.VMEM((2 * n_pairs, V7X_LANES, tq), F32),
                        pltpu.VMEM((n_pairs, tq, V7X_LANES), BF16)],
        compiler_params=_params(2),
        name="fox_attn",
    )(x2, gate, g_post, q, k, vt, augq, augk, w_out.astype(BF16))


def kernel(x, c, w_cond, b_cond, norm_pre, norm_post, w_ffn_in, w_ffn_out, fox_w_in, fox_b_f, fox_w_out, sconv_w_in, sconv_conv_w, sconv_w_out, lru_w_in, lru_conv_w, lru_conv_b, lru_w_a, lru_b_a, lru_w_x, lru_b_x, lru_lambda, lru_w_out):
    bsz, seq, d = x.shape
    depth = w_cond.shape[0]
    assert d % V7X_LANES == 0 and V7X_LANES == 2 * FOX_HEAD_DIM
    assert seq % min(ROW_TILE, seq) == 0 and seq % min(ATTN_TILE, seq) == 0

    mod = _modulation(c, w_cond, b_cond).reshape(depth, bsz, N_SUB, 3, 1, d)
    x2 = x.reshape(bsz * seq, d)
    w_ffn_in16 = w_ffn_in
    w_ffn_out16 = w_ffn_out

    for i in range(depth):
        def mods(s):
            return mod[i, :, s, 0], mod[i, :, s, 1], mod[i, :, s, 2]

        def gains(s):
            return norm_pre[i, s].reshape(1, d), norm_post[i, s].reshape(1, d)

        shift, scale, gate = mods(0)
        x2 = _ffn(x2, seq, shift, scale, gate, *gains(0), w_ffn_in16, w_ffn_out16, (i, 0))

        shift, scale, gate = mods(1)
        g_pre, g_post = gains(1)
        kind, j = i % N_MIXERS, i // N_MIXERS
        if kind == 0:
            q, k, vt, augq, augk = _fox_proj(x2, bsz, seq, shift, scale, g_pre, fox_w_in[j], fox_b_f[j])
            x2 = _fox_attn(x2, bsz, seq, gate, g_post, q, k, vt, augq, augk, fox_w_out[j])
        elif kind == 1:
            x2 = _sconv(x2, seq, shift, scale, gate, g_pre, g_post, sconv_w_in[j], sconv_conv_w[j],
                        sconv_w_out[j])
        else:
            x2 = _lru(x2, seq, shift, scale, gate, g_pre, g_post, lru_w_in[j], lru_conv_w[j],
                      lru_conv_b[j], lru_w_a[j], lru_b_a[j], lru_w_x[j], lru_b_x[j], lru_lambda[j],
                      lru_w_out[j])

        shift, scale, gate = mods(2)
        x2 = _ffn(x2, seq, shift, scale, gate, *gains(2), w_ffn_in16, w_ffn_out16, (i, 1))

    return x2.reshape(bsz, seq, d)
```

```python
import functools

import numpy as np
import jax
import jax.numpy as jnp
from jax import lax
from jax.experimental import pallas as pl
from jax.experimental.pallas import tpu as pltpu

N_SUB = 3
N_MIXERS = 3
FFN_RES_WEIGHT = 0.5
RMS_EPS = 1e-6
FOX_HEAD_DIM = 64
LRU_C = 8.0
LOG2E = 1.4426950408889634

V7X_LANES = 128
V7X_SUBLANES = 8
V7X_MXU_DIM = 256
V7X_VMEM_BYTES = 64 * 1024 * 1024
VMEM_LIMIT_BYTES = V7X_VMEM_BYTES - 8 * 1024 * 1024

ROW_TILE = 512
PROJ_ROW_TILE = 1024
ATTN_TILE = 256
FFN_CHUNK = V7X_MXU_DIM
FFN_ROW_TILE = 512
FFN_SUBTILE = 256
AUG_PER_HEAD = 6
ATTN_GROUP_PAIRS = 8
MOD_COL_TILE = 1536

BF16 = jnp.bfloat16
F32 = jnp.float32


def _params(n_axes):
    return pltpu.CompilerParams(dimension_semantics=("arbitrary",) * n_axes,
                                vmem_limit_bytes=VMEM_LIMIT_BYTES)


def _resident(shape):
    return pl.BlockSpec(shape, lambda *_: (0,) * len(shape), pipeline_mode=pl.Buffered(1))


def _rms(x):
    return x * lax.rsqrt(jnp.mean(x * x, axis=-1, keepdims=True) + RMS_EPS)


def _pre(x, gpre_ref, scale_ref, shift_ref):
    return _rms(x) * (gpre_ref[...] * (1.0 + scale_ref[0])) + shift_ref[0]


def _residual(x, y, gate_ref, gpost_ref, weight=1.0):
    return x + _rms(y) * ((weight * gate_ref[0]) * gpost_ref[...])


def _split3(x):
    hi = x.astype(BF16)
    r1 = x - hi.astype(F32)
    mid = r1.astype(BF16)
    lo = (r1 - mid.astype(F32)).astype(BF16)
    return hi, mid, lo


def _mod_kernel(c_ref, w_ref, b_ref, o_ref):
    c = c_ref[...]
    c_act = (c * jax.nn.sigmoid(c)).astype(BF16)
    o_ref[0] = jnp.dot(c_act, w_ref[0].astype(BF16), preferred_element_type=F32) + b_ref[0]


def _modulation(c, w_cond, b_cond):
    depth, d, n = w_cond.shape
    bsz = c.shape[0]
    tn = MOD_COL_TILE if n % MOD_COL_TILE == 0 else n
    return pl.pallas_call(
        _mod_kernel,
        grid=(depth, n // tn),
        in_specs=[
            pl.BlockSpec((bsz, d), lambda i, j: (0, 0)),
            pl.BlockSpec((1, d, tn), lambda i, j: (i, 0, j)),
            pl.BlockSpec((1, 1, tn), lambda i, j: (i, 0, j)),
        ],
        out_specs=pl.BlockSpec((1, bsz, tn), lambda i, j: (i, 0, j)),
        out_shape=jax.ShapeDtypeStruct((depth, bsz, n), F32),
        compiler_params=_params(2),
        name="adaln_mod",
    )(c, w_cond, b_cond.reshape(depth, 1, n))


def _ffn_kernel(x_ref, shift_ref, scale_ref, gate_ref, gpre_ref, gpost_ref, win_ref, wout_ref,
                o_ref, act_ref):
    f = wout_ref.shape[0]
    fc = FFN_CHUNK
    tm = x_ref.shape[0]
    for r in range(0, tm, FFN_SUBTILE):
        rows = slice(r, r + FFN_SUBTILE)
        x = x_ref[rows, :]
        h = _pre(x, gpre_ref, scale_ref, shift_ref)
        for c in range(f // fc):
            g = jnp.dot(h, win_ref[:, c * fc:(c + 1) * fc], preferred_element_type=F32)
            u = jnp.dot(h, win_ref[:, f + c * fc:f + (c + 1) * fc], preferred_element_type=F32)
            act_ref[rows, c * fc:(c + 1) * fc] = (g * jax.nn.sigmoid(g) * u).astype(BF16)
        y = jnp.dot(act_ref[rows, :].astype(F32), wout_ref[...], preferred_element_type=F32)
        o_ref[rows, :] = _residual(x, y, gate_ref, gpost_ref, FFN_RES_WEIGHT)


def _row_specs(tm, d, tiles_per_batch):
    row = pl.BlockSpec((tm, d), lambda i: (i, 0))
    mod = pl.BlockSpec((1, 1, d), lambda i: (i // tiles_per_batch, 0, 0))
    vec = pl.BlockSpec((1, d), lambda i: (0, 0))
    return row, mod, vec


def _stacked_resident(stacked, index):
    lead = len(index)
    shape = (None,) * lead + tuple(stacked.shape[lead:])
    return pl.BlockSpec(shape, lambda *_: tuple(index) + (0,) * (stacked.ndim - lead),
                        pipeline_mode=pl.Buffered(1))


def _ffn(x2, seq, shift, scale, gate, g_pre, g_post, w_in, w_out, index):
    t, d = x2.shape
    tm = min(FFN_ROW_TILE, seq)
    f = w_out.shape[-2]
    assert f % FFN_CHUNK == 0 and tm % FFN_SUBTILE == 0
    row, mod, vec = _row_specs(tm, d, seq // tm)
    return pl.pallas_call(
        _ffn_kernel,
        grid=(t // tm,),
        in_specs=[row, mod, mod, mod, vec, vec, _stacked_resident(w_in, index),
                  _stacked_resident(w_out, index)],
        out_specs=row,
        out_shape=jax.ShapeDtypeStruct((t, d), F32),
        scratch_shapes=[pltpu.VMEM((tm, f), BF16)],
        compiler_params=_params(1),
        name="ffn",
    )(x2, shift, scale, gate, g_pre, g_post, w_in, w_out)


def _causal_conv(u, hist_ref, w_ref):
    tm = u.shape[0]
    kw = w_ref.shape[0]
    hist_ref[V7X_SUBLANES:, :] = u
    out = w_ref[kw - 1:kw] * u
    for k in range(1, kw):
        out = out + w_ref[kw - 1 - k:kw - k] * hist_ref[V7X_SUBLANES - k:V7X_SUBLANES - k + tm, :]
    hist_ref[:V7X_SUBLANES, :] = u[tm - V7X_SUBLANES:]
    return out


def _sconv_kernel(tiles_per_batch, x_ref, shift_ref, scale_ref, gate_ref, gpre_ref, gpost_ref,
                  win_ref, convw_ref, wout_ref, o_ref, hist_ref):
    d = x_ref.shape[1]

    @pl.when(pl.program_id(0) % tiles_per_batch == 0)
    def _():
        hist_ref[:V7X_SUBLANES, :] = jnp.zeros((V7X_SUBLANES, d), F32)

    sub = hist_ref.shape[0] - V7X_SUBLANES
    for r in range(0, x_ref.shape[0], sub):
        rows = slice(r, r + sub)
        x = x_ref[rows, :]
        h = _pre(x, gpre_ref, scale_ref, shift_ref).astype(BF16)
        c_gate = jnp.dot(h, win_ref[:, d:2 * d], preferred_element_type=F32)
        xv = jnp.dot(h, win_ref[:, 2 * d:], preferred_element_type=F32)
        u = c_gate * xv
        conv = _causal_conv(u, hist_ref, convw_ref)
        b_gate = jnp.dot(h, win_ref[:, :d], preferred_element_type=F32)
        y = jnp.dot((b_gate * conv).astype(BF16), wout_ref[...], preferred_element_type=F32)
        o_ref[rows, :] = _residual(x, y, gate_ref, gpost_ref)


def _sconv(x2, seq, shift, scale, gate, g_pre, g_post, w_in, conv_w, w_out):
    t, d = x2.shape
    tm = min(PROJ_ROW_TILE, seq)
    sub = min(ROW_TILE, tm)
    assert tm % sub == 0
    row, mod, vec = _row_specs(tm, d, seq // tm)
    return pl.pallas_call(
        functools.partial(_sconv_kernel, seq // tm),
        grid=(t // tm,),
        in_specs=[row, mod, mod, mod, vec, vec, _resident(w_in.shape), _resident(conv_w.shape),
                  _resident(w_out.shape)],
        out_specs=row,
        out_shape=jax.ShapeDtypeStruct((t, d), F32),
        scratch_shapes=[pltpu.VMEM((V7X_SUBLANES + sub, d), F32)],
        compiler_params=_params(1),
        name="sconv_mixer",
    )(x2, shift, scale, gate, g_pre, g_post, w_in.astype(BF16), conv_w, w_out.astype(BF16))


def _linear_scan(a, b, h_prev):
    n, c = a.shape
    grouped = (n // V7X_SUBLANES, V7X_SUBLANES, c)
    sub = lax.broadcasted_iota(jnp.int32, grouped, 1)
    a = a.reshape(grouped)
    b = b.reshape(grouped)
    d = 1
    while d < V7X_SUBLANES:
        keep = sub >= d
        b = jnp.where(keep, b + a * pltpu.roll(b, d, axis=1), b)
        a = jnp.where(keep, a * pltpu.roll(a, d, axis=1), a)
        d *= 2
    a = a.reshape(n, c)
    b = b.reshape(n, c)
    groups = []
    for r in range(0, n, V7X_SUBLANES):
        h_group = a[r:r + V7X_SUBLANES] * h_prev + b[r:r + V7X_SUBLANES]
        groups.append(h_group)
        h_prev = h_group[V7X_SUBLANES - 1:]
    return jnp.concatenate(groups, axis=0)


def _lru_kernel(tiles_per_batch, x_ref, shift_ref, scale_ref, gate_ref, gpre_ref, gpost_ref,
                win_ref, convw_ref, convb_ref, wgate_ref, ba_ref, bx_ref, lam_ref, wout_ref,
                o_ref, xhist_ref, hcarry_ref):
    tm = x_ref.shape[0]
    w = wout_ref.shape[0]
    n_groups, gk, two_gn = wgate_ref.shape
    gn = two_gn // 2

    @pl.when(pl.program_id(0) % tiles_per_batch == 0)
    def _():
        xhist_ref[:V7X_SUBLANES, :] = jnp.zeros((V7X_SUBLANES, w), F32)
        hcarry_ref[...] = jnp.zeros(hcarry_ref.shape, F32)

    x = x_ref[...]
    h = _pre(x, gpre_ref, scale_ref, shift_ref).astype(BF16)
    xraw = jnp.dot(h, win_ref[:, w:], preferred_element_type=F32)
    xb = _causal_conv(xraw, xhist_ref, convw_ref) + convb_ref[...]

    xb16 = xb.astype(BF16)
    r_parts, i_parts = [], []
    for g in range(n_groups):
        ri = jnp.dot(xb16[:, g * gk:(g + 1) * gk], wgate_ref[g], preferred_element_type=F32)
        r_parts.append(ri[:, :gn])
        i_parts.append(ri[:, gn:])
    r = jax.nn.sigmoid(jnp.concatenate(r_parts, axis=1) + ba_ref[...])
    i = jax.nn.sigmoid(jnp.concatenate(i_parts, axis=1) + bx_ref[...])

    neg_lam = -lam_ref[...]
    softplus = jnp.maximum(neg_lam, 0.0) + jnp.log1p(jnp.exp(-jnp.abs(neg_lam)))
    log_a = (-LRU_C * softplus) * r
    a = jnp.exp(log_a)
    mult = jnp.sqrt(-jnp.tanh(log_a) * (1.0 + a * a))
    hs = _linear_scan(a, mult * (i * xb), hcarry_ref[V7X_SUBLANES - 1:])
    hcarry_ref[...] = hs[tm - V7X_SUBLANES:]

    gate_br = jnp.dot(h, win_ref[:, :w], preferred_element_type=F32)
    y = jnp.dot((hs * jax.nn.gelu(gate_br)).astype(BF16), wout_ref[...], preferred_element_type=F32)
    o_ref[...] = _residual(x, y, gate_ref, gpost_ref)


def _lru_gate_weights(w_a, w_x):
    nb, bd, _ = w_a.shape
    per = V7X_MXU_DIM // bd
    assert nb % per == 0

    def dense(wb):
        wg = wb.reshape(nb // per, per, bd, bd)
        eye = jnp.eye(per, dtype=wb.dtype)
        return jnp.einsum("gpij,pq->gpiqj", wg, eye).reshape(nb // per, per * bd, per * bd)

    return jnp.concatenate([dense(w_a), dense(w_x)], axis=-1).astype(BF16)


def _lru(x2, seq, shift, scale, gate, g_pre, g_post, w_in, conv_w, conv_b, w_a, b_a, w_x, b_x, lam,
         w_out):
    t, d = x2.shape
    w = w_out.shape[0]
    tm = min(ROW_TILE, seq)
    row, mod, vec = _row_specs(tm, d, seq // tm)
    wvec = pl.BlockSpec((1, w), lambda i: (0, 0))
    w_gate = _lru_gate_weights(w_a, w_x)
    return pl.pallas_call(
        functools.partial(_lru_kernel, seq // tm),
        grid=(t // tm,),
        in_specs=[row, mod, mod, mod, vec, vec, _resident(w_in.shape), _resident(conv_w.shape), wvec,
                  _resident(w_gate.shape), wvec, wvec, wvec, _resident(w_out.shape)],
        out_specs=row,
        out_shape=jax.ShapeDtypeStruct((t, d), F32),
        scratch_shapes=[pltpu.VMEM((V7X_SUBLANES + tm, w), F32), pltpu.VMEM((V7X_SUBLANES, w), F32)],
        compiler_params=_params(1),
        name="lru_mixer",
    )(x2, shift, scale, gate, g_pre, g_post, w_in.astype(BF16), conv_w, conv_b.reshape(1, w), w_gate,
      b_a.reshape(1, w), b_x.reshape(1, w), lam.reshape(1, w), w_out.astype(BF16))


def _fox_proj_kernel(tiles_per_batch, n_heads, x_ref, shift_ref, scale_ref, gpre_ref, wqk_ref, wvt_ref, wf_ref,
                     bf_ref, tri_ref, pqk_ref, oneq_ref, onek_ref,
                     q_ref, k_ref, vt_ref, augq_ref, augk_ref, carry_ref):
    d = x_ref.shape[1]
    n_pairs = q_ref.shape[1]
    tk = vt_ref.shape[4]

    @pl.when(pl.program_id(0) % tiles_per_batch == 0)
    def _():
        carry_ref[...] = jnp.zeros(carry_ref.shape, F32)

    sub = tri_ref.shape[0]
    for r in range(0, x_ref.shape[0], sub):
        rows = slice(r, r + sub)
        h = _pre(x_ref[rows, :], gpre_ref, scale_ref, shift_ref).astype(BF16)
        q = jnp.dot(h, wqk_ref[:, :d], preferred_element_type=F32) * (FOX_HEAD_DIM ** -0.5 * LOG2E)
        k = jnp.dot(h, wqk_ref[:, d:], preferred_element_type=F32)
        vt = lax.dot_general(wvt_ref[...], h, (((1,), (1,)), ((), ())), preferred_element_type=F32)
        for p in range(n_pairs):
            sl = slice(p * V7X_LANES, (p + 1) * V7X_LANES)
            q_ref[0, p, rows, :] = q[:, sl].astype(BF16)
            k_ref[0, p, rows, :] = k[:, sl].astype(BF16)
            for j in range(sub // tk):
                vt_ref[0, p, r // tk + j] = vt[sl, j * tk:(j + 1) * tk].astype(BF16)

        f_logit = jnp.dot(h, wf_ref[...], preferred_element_type=F32) + bf_ref[...]
        log_f = jnp.minimum(f_logit, 0.0) - jnp.log1p(jnp.exp(-jnp.abs(f_logit)))
        lanes = lax.broadcasted_iota(jnp.int32, log_f.shape, 1)
        log_f = jnp.where(lanes < n_heads, log_f, 0.0)
        sums = jnp.dot(tri_ref[...], jnp.concatenate(_split3(log_f), axis=1), preferred_element_type=F32)
        cum = (carry_ref[V7X_SUBLANES - 1:] + sums[:, :V7X_LANES] + sums[:, V7X_LANES:2 * V7X_LANES]
               + sums[:, 2 * V7X_LANES:])
        carry_ref[...] = cum[sub - V7X_SUBLANES:]

        parts = jnp.concatenate(_split3(cum * LOG2E), axis=1)
        aug = jnp.dot(parts, pqk_ref[...], preferred_element_type=F32)
        augq_ref[rows, :] = (aug[:, :V7X_LANES] + oneq_ref[...]).astype(BF16)
        augk_ref[rows, :] = (aug[:, V7X_LANES:] + onek_ref[...]).astype(BF16)


def _aug_constants(n_heads):
    assert n_heads * AUG_PER_HEAD <= V7X_LANES
    pq = np.zeros((3 * V7X_LANES, V7X_LANES), np.float32)
    pk = np.zeros((3 * V7X_LANES, V7X_LANES), np.float32)
    oneq = np.zeros((1, V7X_LANES), np.float32)
    onek = np.zeros((1, V7X_LANES), np.float32)
    for h in range(n_heads):
        for j in range(3):
            pq[j * V7X_LANES + h, AUG_PER_HEAD * h + j] = 1.0
            pk[j * V7X_LANES + h, AUG_PER_HEAD * h + 3 + j] = -1.0
            oneq[0, AUG_PER_HEAD * h + 3 + j] = 1.0
            onek[0, AUG_PER_HEAD * h + j] = 1.0
    return jnp.asarray(np.concatenate([pq, pk], axis=1), BF16), jnp.asarray(oneq), jnp.asarray(onek)


def _fox_proj(x2, bsz, seq, shift, scale, g_pre, w_in, b_f):
    t, d = x2.shape
    n_heads = d // FOX_HEAD_DIM
    n_pairs = d // V7X_LANES
    tm = min(PROJ_ROW_TILE, seq)
    sub = min(ROW_TILE, tm)
    tiles_per_batch = seq // tm
    row, mod, vec = _row_specs(tm, d, tiles_per_batch)
    tk = min(ATTN_TILE, seq)
    assert tm % sub == 0 and sub % tk == 0
    w_qk = w_in[:, :2 * d].astype(BF16)
    w_vt = w_in[:, 2 * d:3 * d].T.astype(BF16)
    w_f = jnp.pad(w_in[:, 3 * d:], ((0, 0), (0, V7X_LANES - n_heads))).astype(BF16)
    b_f = jnp.pad(b_f, (0, V7X_LANES - n_heads)).reshape(1, V7X_LANES)
    tri = jnp.asarray(np.tril(np.ones((sub, sub), np.float32)), BF16)
    pqk, oneq, onek = _aug_constants(n_heads)
    pair_spec = pl.BlockSpec((1, n_pairs, tm, V7X_LANES),
                             lambda i: (i // tiles_per_batch, 0, i % tiles_per_batch, 0))
    vt_spec = pl.BlockSpec((1, n_pairs, tm // tk, V7X_LANES, tk),
                           lambda i: (i // tiles_per_batch, 0, i % tiles_per_batch, 0, 0))
    aug_spec = pl.BlockSpec((tm, V7X_LANES), lambda i: (i, 0))
    pair_shape = jax.ShapeDtypeStruct((bsz, n_pairs, seq, V7X_LANES), BF16)
    vt_shape = jax.ShapeDtypeStruct((bsz, n_pairs, seq // tk, V7X_LANES, tk), BF16)
    aug_shape = jax.ShapeDtypeStruct((t, V7X_LANES), BF16)
    return pl.pallas_call(
        functools.partial(_fox_proj_kernel, tiles_per_batch, n_heads),
        grid=(t // tm,),
        in_specs=[row, mod, mod, vec, _resident(w_qk.shape), _resident(w_vt.shape), _resident(w_f.shape),
                  _resident(b_f.shape), _resident(tri.shape), _resident(pqk.shape),
                  _resident(oneq.shape), _resident(onek.shape)],
        out_specs=[pair_spec, pair_spec, vt_spec, aug_spec, aug_spec],
        out_shape=[pair_shape, pair_shape, vt_shape, aug_shape, aug_shape],
        scratch_shapes=[pltpu.VMEM((V7X_SUBLANES, V7X_LANES), F32)],
        compiler_params=_params(1),
        name="fox_proj",
    )(x2, shift, scale, g_pre, w_qk, w_vt, w_f, b_f, tri, pqk, oneq, onek)


def _fox_attn_kernel(group_pairs, x_ref, gate_ref, gpost_ref, q_ref, k_ref, vt_ref, augq_ref, augk_ref,
                     wout_ref, o_ref, lhs_ref, stage_ref, mx_ref, alpha_ref, acc_ref, heads_ref):
    tq = x_ref.shape[0]
    n_pairs = q_ref.shape[1]
    qi = pl.program_id(1)
    lanes = lax.broadcasted_iota(jnp.int32, (tq, V7X_LANES), 1)
    half = lanes // FOX_HEAD_DIM
    causal_t = (lax.broadcasted_iota(jnp.int32, (tq, tq), 0)
                <= lax.broadcasted_iota(jnp.int32, (tq, tq), 1))
    augq = augq_ref[...]
    contract_last = (((1,), (1,)), ((), ()))
    zero16 = jnp.zeros((), BF16)
    ones_rows = jnp.ones((FOX_HEAD_DIM, tq), BF16)
    n_slabs = tq // V7X_SUBLANES

    def key_rows(kb):
        return pl.ds(pl.multiple_of(kb * tq, tq), tq)

    def over_rows(vec, n_rows):
        return jnp.concatenate([vec] * (n_rows // V7X_SUBLANES), axis=0)

    def stage(pairs):
        for p in pairs:
            q_pair = q_ref[0, p]
            for e in range(2):
                h = 2 * p + e
                own_aug = (lanes >= AUG_PER_HEAD * h) & (lanes < AUG_PER_HEAD * (h + 1))
                lhs_ref[h] = jnp.concatenate(
                    [jnp.where(half == e, q_pair, zero16), jnp.where(own_aug, augq, zero16)], axis=1)
                mx_ref[h] = jnp.full((V7X_SUBLANES, tq), -jnp.inf, F32)
                acc_ref[h] = jnp.zeros((V7X_LANES, tq), F32)

    def attend(pairs, kb, n_blocks, masked, slot):
        ones = jnp.concatenate([ones_rows] * n_blocks, axis=1)
        for p in pairs:
            keys = [jnp.concatenate([k_ref[0, p, key_rows(kb + j), :], augk_ref[key_rows(kb + j), :]], axis=1)
                    for j in range(n_blocks)]
            for e in range(2):
                h = 2 * p + e
                hh = h - 2 * pairs[0]
                slabs = []
                for j in range(n_blocks):
                    s = lax.dot_general(keys[j], lhs_ref[h], contract_last, preferred_element_type=F32)
                    if masked:
                        s = jnp.where(causal_t, s, -jnp.inf)
                    stage_ref[slot, hh, j] = s
                    slabs += [s[i * V7X_SUBLANES:(i + 1) * V7X_SUBLANES] for i in range(n_slabs)]
                while len(slabs) > 1:
                    slabs = [jnp.maximum(a, b) for a, b in zip(slabs[0::2], slabs[1::2])]
                m_old = mx_ref[h]
                m_new = jnp.maximum(m_old, jnp.broadcast_to(jnp.max(slabs[0], axis=0, keepdims=True),
                                                            (V7X_SUBLANES, tq)))
                mx_ref[h] = m_new
                alpha_ref[h] = jnp.exp2(m_old - m_new)
        for p in pairs:
            vt_pair = jnp.concatenate([vt_ref[0, p, kb + j] for j in range(n_blocks)], axis=1)
            for e in range(2):
                h = 2 * p + e
                hh = h - 2 * pairs[0]
                m_rows = over_rows(mx_ref[h], tq)
                prob = jnp.concatenate(
                    [jnp.exp2(stage_ref[slot, hh, j] - m_rows).astype(BF16) for j in range(n_blocks)], axis=0)
                vt_own = (jnp.concatenate([vt_pair[:FOX_HEAD_DIM], ones], axis=0) if e == 0 else
                          jnp.concatenate([ones, vt_pair[FOX_HEAD_DIM:]], axis=0))
                acc_ref[h] = (over_rows(alpha_ref[h], V7X_LANES) * acc_ref[h]
                              + jnp.dot(vt_own, prob, preferred_element_type=F32))

    def finalize(pairs):
        for p in pairs:
            a_even = acc_ref[2 * p]
            a_odd = acc_ref[2 * p + 1]
            numer = jnp.concatenate([a_even[:FOX_HEAD_DIM], a_odd[FOX_HEAD_DIM:]], axis=0)
            denom = jnp.concatenate([a_even[FOX_HEAD_DIM:], a_odd[:FOX_HEAD_DIM]], axis=0)
            heads_ref[p] = (numer / denom).T.astype(BF16)

    for g in range(n_pairs // group_pairs):
        pairs = range(g * group_pairs, (g + 1) * group_pairs)
        stage(pairs)
        attend(pairs, qi, 1, True, qi % 2)

        def body(i, c, pairs=pairs):
            attend(pairs, 2 * i, 2, False, i % 2)
            return c

        lax.fori_loop(0, qi // 2, body, 0)

        @pl.when(qi % 2 == 1)
        def _(pairs=pairs):
            attend(pairs, qi - 1, 1, False, qi % 2)

        finalize(pairs)

    o = jnp.concatenate([heads_ref[p] for p in range(n_pairs)], axis=1)
    y = jnp.dot(o, wout_ref[...], preferred_element_type=F32)
    o_ref[...] = _residual(x_ref[...], y, gate_ref, gpost_ref)


def _fox_attn(x2, bsz, seq, gate, g_post, q, k, vt, augq, augk, w_out):
    t, d = x2.shape
    n_pairs = d // V7X_LANES
    tq = min(ATTN_TILE, seq)
    nq = seq // tq
    group_pairs = min(ATTN_GROUP_PAIRS, n_pairs)
    assert n_pairs % group_pairs == 0
    row = pl.BlockSpec((tq, d), lambda b, i: (b * nq + i, 0))
    mod = pl.BlockSpec((1, 1, d), lambda b, i: (b, 0, 0))
    vec = pl.BlockSpec((1, d), lambda b, i: (0, 0))
    q_spec = pl.BlockSpec((1, n_pairs, tq, V7X_LANES), lambda b, i: (b, 0, i, 0))
    k_spec = pl.BlockSpec((1, n_pairs, seq, V7X_LANES), lambda b, i: (b, 0, 0, 0))
    vt_spec = pl.BlockSpec((1, n_pairs, nq, V7X_LANES, tq), lambda b, i: (b, 0, 0, 0, 0))
    augq_spec = pl.BlockSpec((tq, V7X_LANES), lambda b, i: (b * nq + i, 0))
    augk_spec = pl.BlockSpec((seq, V7X_LANES), lambda b, i: (b, 0))
    return pl.pallas_call(
        functools.partial(_fox_attn_kernel, group_pairs),
        grid=(bsz, nq),
        in_specs=[row, mod, vec, q_spec, k_spec, vt_spec, augq_spec, augk_spec, _resident(w_out.shape)],
        out_specs=row,
        out_shape=jax.ShapeDtypeStruct((t, d), F32),
        scratch_shapes=[pltpu.VMEM((2 * n_pairs, tq, 2 * V7X_LANES), BF16),
                        pltpu.VMEM((2, 2 * group_pairs, 2, tq, tq), F32),
                        pltpu.VMEM((2 * n_pairs, V7X_SUBLANES, tq), F32),
                        pltpu.VMEM((2 * n_pairs, V7X_SUBLANES, tq), F32),
                        pltpu.VMEM((2 * n_pairs, V7X_LANES, tq), F32),
                        pltpu.VMEM((n_pairs, tq, V7X_LANES), BF16)],
        compiler_params=_params(2),
        name="fox_attn",
    )(x2, gate, g_post, q, k, vt, augq, augk, w_out.astype(BF16))


def kernel(x, c, w_cond, b_cond, norm_pre, norm_post, w_ffn_in, w_ffn_out, fox_w_in, fox_b_f, fox_w_out, sconv_w_in, sconv_conv_w, sconv_w_out, lru_w_in, lru_conv_w, lru_conv_b, lru_w_a, lru_b_a, lru_w_x, lru_b_x, lru_lambda, lru_w_out):
    bsz, seq, d = x.shape
    depth = w_cond.shape[0]
    assert d % V7X_LANES == 0 and V7X_LANES == 2 * FOX_HEAD_DIM
    assert all(seq % min(tile, seq) == 0 for tile in (ROW_TILE, PROJ_ROW_TILE, FFN_ROW_TILE, ATTN_TILE))

    mod = _modulation(c, w_cond, b_cond).reshape(depth, bsz, N_SUB, 3, 1, d)
    x2 = x.reshape(bsz * seq, d)
    w_ffn_in16 = w_ffn_in
    w_ffn_out16 = w_ffn_out

    for i in range(depth):
        def mods(s):
            return mod[i, :, s, 0], mod[i, :, s, 1], mod[i, :, s, 2]

        def gains(s):
            return norm_pre[i, s].reshape(1, d), norm_post[i, s].reshape(1, d)

        shift, scale, gate = mods(0)
        x2 = _ffn(x2, seq, shift, scale, gate, *gains(0), w_ffn_in16, w_ffn_out16, (i, 0))

        shift, scale, gate = mods(1)
        g_pre, g_post = gains(1)
        kind, j = i % N_MIXERS, i // N_MIXERS
        if kind == 0:
            q, k, vt, augq, augk = _fox_proj(x2, bsz, seq, shift, scale, g_pre, fox_w_in[j], fox_b_f[j])
            x2 = _fox_attn(x2, bsz, seq, gate, g_post, q, k, vt, augq, augk, fox_w_out[j])
        elif kind == 1:
            x2 = _sconv(x2, seq, shift, scale, gate, g_pre, g_post, sconv_w_in[j], sconv_conv_w[j],
                        sconv_w_out[j])
        else:
            x2 = _lru(x2, seq, shift, scale, gate, g_pre, g_post, lru_w_in[j], lru_conv_w[j],
                      lru_conv_b[j], lru_w_a[j], lru_b_a[j], lru_w_x[j], lru_b_x[j], lru_lambda[j],
                      lru_w_out[j])

        shift, scale, gate = mods(2)
        x2 = _ffn(x2, seq, shift, scale, gate, *gains(2), w_ffn_in16, w_ffn_out16, (i, 1))

    return x2.reshape(bsz, seq, d)
```

```python
import functools

import numpy as np
import jax
import jax.numpy as jnp
from jax import lax
from jax.experimental import pallas as pl
from jax.experimental.pallas import tpu as pltpu

N_SUB = 3
N_MIXERS = 3
FFN_RES_WEIGHT = 0.5
RMS_EPS = 1e-6
FOX_HEAD_DIM = 64
LRU_C = 8.0
LOG2E = 1.4426950408889634

V7X_LANES = 128
V7X_SUBLANES = 8
V7X_MXU_DIM = 256
V7X_VMEM_BYTES = 64 * 1024 * 1024
VMEM_LIMIT_BYTES = V7X_VMEM_BYTES - 8 * 1024 * 1024

ROW_TILE = 512
PROJ_ROW_TILE = 1024
ATTN_TILE = 256
FFN_CHUNK = V7X_MXU_DIM
FFN_ROW_TILE = 512
FFN_SUBTILE = 256
AUG_PER_HEAD = 6
ATTN_GROUP_PAIRS = 8
MOD_COL_TILE = 1536

BF16 = jnp.bfloat16
F32 = jnp.float32


def _params(n_axes):
    return pltpu.CompilerParams(dimension_semantics=("arbitrary",) * n_axes,
                                vmem_limit_bytes=VMEM_LIMIT_BYTES)


def _resident(shape):
    return pl.BlockSpec(shape, lambda *_: (0,) * len(shape), pipeline_mode=pl.Buffered(1))


def _rms(x):
    return x * lax.rsqrt(jnp.mean(x * x, axis=-1, keepdims=True) + RMS_EPS)


def _pre(x, gpre_ref, scale_ref, shift_ref):
    return _rms(x) * (gpre_ref[...] * (1.0 + scale_ref[0])) + shift_ref[0]


def _residual(x, y, gate_ref, gpost_ref, weight=1.0):
    return x + _rms(y) * ((weight * gate_ref[0]) * gpost_ref[...])


def _split3(x):
    hi = x.astype(BF16)
    r1 = x - hi.astype(F32)
    mid = r1.astype(BF16)
    lo = (r1 - mid.astype(F32)).astype(BF16)
    return hi, mid, lo


def _mod_kernel(c_ref, w_ref, b_ref, o_ref):
    c = c_ref[...]
    c_act = (c * jax.nn.sigmoid(c)).astype(BF16)
    o_ref[0] = jnp.dot(c_act, w_ref[0].astype(BF16), preferred_element_type=F32) + b_ref[0]


def _modulation(c, w_cond, b_cond):
    depth, d, n = w_cond.shape
    bsz = c.shape[0]
    tn = MOD_COL_TILE if n % MOD_COL_TILE == 0 else n
    return pl.pallas_call(
        _mod_kernel,
        grid=(depth, n // tn),
        in_specs=[
            pl.BlockSpec((bsz, d), lambda i, j: (0, 0)),
            pl.BlockSpec((1, d, tn), lambda i, j: (i, 0, j)),
            pl.BlockSpec((1, 1, tn), lambda i, j: (i, 0, j)),
        ],
        out_specs=pl.BlockSpec((1, bsz, tn), lambda i, j: (i, 0, j)),
        out_shape=jax.ShapeDtypeStruct((depth, bsz, n), F32),
        compiler_params=_params(2),
        name="adaln_mod",
    )(c, w_cond, b_cond.reshape(depth, 1, n))


def _ffn_kernel(x_ref, shift_ref, scale_ref, gate_ref, gpre_ref, gpost_ref, win_ref, wout_ref,
                o_ref, act_ref):
    f = wout_ref.shape[0]
    fc = FFN_CHUNK
    tm = x_ref.shape[0]
    for r in range(0, tm, FFN_SUBTILE):
        rows = slice(r, r + FFN_SUBTILE)
        x = x_ref[rows, :]
        h = _pre(x, gpre_ref, scale_ref, shift_ref)
        for c in range(f // fc):
            g = jnp.dot(h, win_ref[:, c * fc:(c + 1) * fc], preferred_element_type=F32)
            u = jnp.dot(h, win_ref[:, f + c * fc:f + (c + 1) * fc], preferred_element_type=F32)
            act_ref[rows, c * fc:(c + 1) * fc] = (g * jax.nn.sigmoid(g) * u).astype(BF16)
        y = jnp.dot(act_ref[rows, :].astype(F32), wout_ref[...], preferred_element_type=F32)
        o_ref[rows, :] = _residual(x, y, gate_ref, gpost_ref, FFN_RES_WEIGHT)


def _row_specs(tm, d, tiles_per_batch):
    row = pl.BlockSpec((tm, d), lambda i: (i, 0))
    mod = pl.BlockSpec((1, 1, d), lambda i: (i // tiles_per_batch, 0, 0))
    vec = pl.BlockSpec((1, d), lambda i: (0, 0))
    return row, mod, vec


def _stacked_resident(stacked, index):
    lead = len(index)
    shape = (None,) * lead + tuple(stacked.shape[lead:])
    return pl.BlockSpec(shape, lambda *_: tuple(index) + (0,) * (stacked.ndim - lead),
                        pipeline_mode=pl.Buffered(1))


def _ffn(x2, seq, shift, scale, gate, g_pre, g_post, w_in, w_out, index):
    t, d = x2.shape
    tm = min(FFN_ROW_TILE, seq)
    f = w_out.shape[-2]
    assert f % FFN_CHUNK == 0 and tm % FFN_SUBTILE == 0
    row, mod, vec = _row_specs(tm, d, seq // tm)
    return pl.pallas_call(
        _ffn_kernel,
        grid=(t // tm,),
        in_specs=[row, mod, mod, mod, vec, vec, _stacked_resident(w_in, index),
                  _stacked_resident(w_out, index)],
        out_specs=row,
        out_shape=jax.ShapeDtypeStruct((t, d), F32),
        scratch_shapes=[pltpu.VMEM((tm, f), BF16)],
        compiler_params=_params(1),
        name="ffn",
    )(x2, shift, scale, gate, g_pre, g_post, w_in, w_out)


def _causal_conv(u, hist_ref, w_ref):
    tm = u.shape[0]
    kw = w_ref.shape[0]
    hist_ref[V7X_SUBLANES:, :] = u
    out = w_ref[kw - 1:kw] * u
    for k in range(1, kw):
        out = out + w_ref[kw - 1 - k:kw - k] * hist_ref[V7X_SUBLANES - k:V7X_SUBLANES - k + tm, :]
    hist_ref[:V7X_SUBLANES, :] = u[tm - V7X_SUBLANES:]
    return out


def _sconv_kernel(tiles_per_batch, x_ref, shift_ref, scale_ref, gate_ref, gpre_ref, gpost_ref,
                  win_ref, convw_ref, wout_ref, o_ref, hist_ref):
    d = x_ref.shape[1]

    @pl.when(pl.program_id(0) % tiles_per_batch == 0)
    def _():
        hist_ref[:V7X_SUBLANES, :] = jnp.zeros((V7X_SUBLANES, d), F32)

    sub = hist_ref.shape[0] - V7X_SUBLANES
    for r in range(0, x_ref.shape[0], sub):
        rows = slice(r, r + sub)
        x = x_ref[rows, :]
        h = _pre(x, gpre_ref, scale_ref, shift_ref)
        c_gate = jnp.dot(h, win_ref[:, d:2 * d], preferred_element_type=F32)
        xv = jnp.dot(h, win_ref[:, 2 * d:], preferred_element_type=F32)
        u = c_gate * xv
        conv = _causal_conv(u, hist_ref, convw_ref)
        b_gate = jnp.dot(h, win_ref[:, :d], preferred_element_type=F32)
        y = jnp.dot(b_gate * conv, wout_ref[...], preferred_element_type=F32)
        o_ref[rows, :] = _residual(x, y, gate_ref, gpost_ref)


def _sconv(x2, seq, shift, scale, gate, g_pre, g_post, w_in, conv_w, w_out):
    t, d = x2.shape
    tm = min(PROJ_ROW_TILE, seq)
    sub = min(ROW_TILE, tm)
    assert tm % sub == 0
    row, mod, vec = _row_specs(tm, d, seq // tm)
    return pl.pallas_call(
        functools.partial(_sconv_kernel, seq // tm),
        grid=(t // tm,),
        in_specs=[row, mod, mod, mod, vec, vec, _resident(w_in.shape), _resident(conv_w.shape),
                  _resident(w_out.shape)],
        out_specs=row,
        out_shape=jax.ShapeDtypeStruct((t, d), F32),
        scratch_shapes=[pltpu.VMEM((V7X_SUBLANES + sub, d), F32)],
        compiler_params=_params(1),
        name="sconv_mixer",
    )(x2, shift, scale, gate, g_pre, g_post, w_in, conv_w, w_out)


def _linear_scan(a, b, h_prev):
    n, c = a.shape
    grouped = (n // V7X_SUBLANES, V7X_SUBLANES, c)
    sub = lax.broadcasted_iota(jnp.int32, grouped, 1)
    a = a.reshape(grouped)
    b = b.reshape(grouped)
    d = 1
    while d < V7X_SUBLANES:
        keep = sub >= d
        b = jnp.where(keep, b + a * pltpu.roll(b, d, axis=1), b)
        a = jnp.where(keep, a * pltpu.roll(a, d, axis=1), a)
        d *= 2
    a = a.reshape(n, c)
    b = b.reshape(n, c)
    groups = []
    for r in range(0, n, V7X_SUBLANES):
        h_group = a[r:r + V7X_SUBLANES] * h_prev + b[r:r + V7X_SUBLANES]
        groups.append(h_group)
        h_prev = h_group[V7X_SUBLANES - 1:]
    return jnp.concatenate(groups, axis=0)


def _lru_kernel(tiles_per_batch, x_ref, shift_ref, scale_ref, gate_ref, gpre_ref, gpost_ref,
                win_ref, convw_ref, convb_ref, wgate_ref, ba_ref, bx_ref, lam_ref, wout_ref,
                o_ref, xhist_ref, hcarry_ref):
    tm = x_ref.shape[0]
    w = wout_ref.shape[0]
    n_groups, gk, two_gn = wgate_ref.shape
    gn = two_gn // 2

    @pl.when(pl.program_id(0) % tiles_per_batch == 0)
    def _():
        xhist_ref[:V7X_SUBLANES, :] = jnp.zeros((V7X_SUBLANES, w), F32)
        hcarry_ref[...] = jnp.zeros(hcarry_ref.shape, F32)

    x = x_ref[...]
    h = _pre(x, gpre_ref, scale_ref, shift_ref).astype(BF16)
    xraw = jnp.dot(h, win_ref[:, w:], preferred_element_type=F32)
    xb = _causal_conv(xraw, xhist_ref, convw_ref) + convb_ref[...]

    xb16 = xb.astype(BF16)
    r_parts, i_parts = [], []
    for g in range(n_groups):
        ri = jnp.dot(xb16[:, g * gk:(g + 1) * gk], wgate_ref[g], preferred_element_type=F32)
        r_parts.append(ri[:, :gn])
        i_parts.append(ri[:, gn:])
    r = jax.nn.sigmoid(jnp.concatenate(r_parts, axis=1) + ba_ref[...])
    i = jax.nn.sigmoid(jnp.concatenate(i_parts, axis=1) + bx_ref[...])

    neg_lam = -lam_ref[...]
    softplus = jnp.maximum(neg_lam, 0.0) + jnp.log1p(jnp.exp(-jnp.abs(neg_lam)))
    log_a = (-LRU_C * softplus) * r
    a = jnp.exp(log_a)
    mult = jnp.sqrt(-jnp.tanh(log_a) * (1.0 + a * a))
    hs = _linear_scan(a, mult * (i * xb), hcarry_ref[V7X_SUBLANES - 1:])
    hcarry_ref[...] = hs[tm - V7X_SUBLANES:]

    gate_br = jnp.dot(h, win_ref[:, :w], preferred_element_type=F32)
    y = jnp.dot((hs * jax.nn.gelu(gate_br)).astype(BF16), wout_ref[...], preferred_element_type=F32)
    o_ref[...] = _residual(x, y, gate_ref, gpost_ref)


def _lru_gate_weights(w_a, w_x):
    nb, bd, _ = w_a.shape
    per = V7X_MXU_DIM // bd
    assert nb % per == 0

    def dense(wb):
        wg = wb.reshape(nb // per, per, bd, bd)
        eye = jnp.eye(per, dtype=wb.dtype)
        return jnp.einsum("gpij,pq->gpiqj", wg, eye).reshape(nb // per, per * bd, per * bd)

    return jnp.concatenate([dense(w_a), dense(w_x)], axis=-1).astype(BF16)


def _lru(x2, seq, shift, scale, gate, g_pre, g_post, w_in, conv_w, conv_b, w_a, b_a, w_x, b_x, lam,
         w_out):
    t, d = x2.shape
    w = w_out.shape[0]
    tm = min(ROW_TILE, seq)
    row, mod, vec = _row_specs(tm, d, seq // tm)
    wvec = pl.BlockSpec((1, w), lambda i: (0, 0))
    w_gate = _lru_gate_weights(w_a, w_x)
    return pl.pallas_call(
        functools.partial(_lru_kernel, seq // tm),
        grid=(t // tm,),
        in_specs=[row, mod, mod, mod, vec, vec, _resident(w_in.shape), _resident(conv_w.shape), wvec,
                  _resident(w_gate.shape), wvec, wvec, wvec, _resident(w_out.shape)],
        out_specs=row,
        out_shape=jax.ShapeDtypeStruct((t, d), F32),
        scratch_shapes=[pltpu.VMEM((V7X_SUBLANES + tm, w), F32), pltpu.VMEM((V7X_SUBLANES, w), F32)],
        compiler_params=_params(1),
        name="lru_mixer",
    )(x2, shift, scale, gate, g_pre, g_post, w_in.astype(BF16), conv_w, conv_b.reshape(1, w), w_gate,
      b_a.reshape(1, w), b_x.reshape(1, w), lam.reshape(1, w), w_out.astype(BF16))


def _fox_proj_kernel(tiles_per_batch, n_heads, x_ref, shift_ref, scale_ref, gpre_ref, win_ref, wvt_ref, wf_ref,
                     bf_ref, tri_ref, pqk_ref, oneq_ref, onek_ref,
                     q_ref, k_ref, vt_ref, augq_ref, augk_ref, carry_ref):
    d = x_ref.shape[1]
    n_pairs = q_ref.shape[1]
    tk = vt_ref.shape[4]

    @pl.when(pl.program_id(0) % tiles_per_batch == 0)
    def _():
        carry_ref[...] = jnp.zeros(carry_ref.shape, F32)

    sub = tri_ref.shape[0]
    for r in range(0, x_ref.shape[0], sub):
        rows = slice(r, r + sub)
        h = _pre(x_ref[rows, :], gpre_ref, scale_ref, shift_ref)
        q = jnp.dot(h, win_ref[:, :d], preferred_element_type=F32) * (FOX_HEAD_DIM ** -0.5 * LOG2E)
        k = jnp.dot(h, win_ref[:, d:2 * d], preferred_element_type=F32)
        vt = lax.dot_general(wvt_ref[...], h, (((1,), (1,)), ((), ())), preferred_element_type=F32)
        for p in range(n_pairs):
            sl = slice(p * V7X_LANES, (p + 1) * V7X_LANES)
            q_ref[0, p, rows, :] = q[:, sl].astype(BF16)
            k_ref[0, p, rows, :] = k[:, sl].astype(BF16)
            for j in range(sub // tk):
                vt_ref[0, p, r // tk + j] = vt[sl, j * tk:(j + 1) * tk].astype(BF16)

        f_logit = jnp.dot(h, wf_ref[...], preferred_element_type=F32) + bf_ref[...]
        log_f = jnp.minimum(f_logit, 0.0) - jnp.log1p(jnp.exp(-jnp.abs(f_logit)))
        lanes = lax.broadcasted_iota(jnp.int32, log_f.shape, 1)
        log_f = jnp.where(lanes < n_heads, log_f, 0.0)
        sums = jnp.dot(tri_ref[...], jnp.concatenate(_split3(log_f), axis=1), preferred_element_type=F32)
        cum = (carry_ref[V7X_SUBLANES - 1:] + sums[:, :V7X_LANES] + sums[:, V7X_LANES:2 * V7X_LANES]
               + sums[:, 2 * V7X_LANES:])
        carry_ref[...] = cum[sub - V7X_SUBLANES:]

        parts = jnp.concatenate(_split3(cum * LOG2E), axis=1)
        aug = jnp.dot(parts, pqk_ref[...], preferred_element_type=F32)
        augq_ref[rows, :] = (aug[:, :V7X_LANES] + oneq_ref[...]).astype(BF16)
        augk_ref[rows, :] = (aug[:, V7X_LANES:] + onek_ref[...]).astype(BF16)


def _aug_constants(n_heads):
    assert n_heads * AUG_PER_HEAD <= V7X_LANES
    pq = np.zeros((3 * V7X_LANES, V7X_LANES), np.float32)
    pk = np.zeros((3 * V7X_LANES, V7X_LANES), np.float32)
    oneq = np.zeros((1, V7X_LANES), np.float32)
    onek = np.zeros((1, V7X_LANES), np.float32)
    for h in range(n_heads):
        for j in range(3):
            pq[j * V7X_LANES + h, AUG_PER_HEAD * h + j] = 1.0
            pk[j * V7X_LANES + h, AUG_PER_HEAD * h + 3 + j] = -1.0
            oneq[0, AUG_PER_HEAD * h + 3 + j] = 1.0
            onek[0, AUG_PER_HEAD * h + j] = 1.0
    return jnp.asarray(np.concatenate([pq, pk], axis=1), BF16), jnp.asarray(oneq), jnp.asarray(onek)


def _fox_proj(x2, bsz, seq, shift, scale, g_pre, w_in_all, j, b_f):
    t, d = x2.shape
    n_heads = d // FOX_HEAD_DIM
    n_pairs = d // V7X_LANES
    tm = min(PROJ_ROW_TILE, seq)
    sub = min(ROW_TILE, tm)
    tiles_per_batch = seq // tm
    row, mod, vec = _row_specs(tm, d, tiles_per_batch)
    tk = min(ATTN_TILE, seq)
    assert tm % sub == 0 and sub % tk == 0
    w_vt = w_in_all[j, :, 2 * d:3 * d].T
    w_f = jnp.pad(w_in_all[j, :, 3 * d:], ((0, 0), (0, V7X_LANES - n_heads)))
    b_f = jnp.pad(b_f, (0, V7X_LANES - n_heads)).reshape(1, V7X_LANES)
    tri = jnp.asarray(np.tril(np.ones((sub, sub), np.float32)), BF16)
    pqk, oneq, onek = _aug_constants(n_heads)
    pair_spec = pl.BlockSpec((1, n_pairs, tm, V7X_LANES),
                             lambda i: (i // tiles_per_batch, 0, i % tiles_per_batch, 0))
    vt_spec = pl.BlockSpec((1, n_pairs, tm // tk, V7X_LANES, tk),
                           lambda i: (i // tiles_per_batch, 0, i % tiles_per_batch, 0, 0))
    aug_spec = pl.BlockSpec((tm, V7X_LANES), lambda i: (i, 0))
    pair_shape = jax.ShapeDtypeStruct((bsz, n_pairs, seq, V7X_LANES), BF16)
    vt_shape = jax.ShapeDtypeStruct((bsz, n_pairs, seq // tk, V7X_LANES, tk), BF16)
    aug_shape = jax.ShapeDtypeStruct((t, V7X_LANES), BF16)
    return pl.pallas_call(
        functools.partial(_fox_proj_kernel, tiles_per_batch, n_heads),
        grid=(t // tm,),
        in_specs=[row, mod, mod, vec, _stacked_resident(w_in_all, (j,)), _resident(w_vt.shape), _resident(w_f.shape),
                  _resident(b_f.shape), _resident(tri.shape), _resident(pqk.shape),
                  _resident(oneq.shape), _resident(onek.shape)],
        out_specs=[pair_spec, pair_spec, vt_spec, aug_spec, aug_spec],
        out_shape=[pair_shape, pair_shape, vt_shape, aug_shape, aug_shape],
        scratch_shapes=[pltpu.VMEM((V7X_SUBLANES, V7X_LANES), F32)],
        compiler_params=_params(1),
        name="fox_proj",
    )(x2, shift, scale, g_pre, w_in_all, w_vt, w_f, b_f, tri, pqk, oneq, onek)


def _fox_attn_kernel(group_pairs, x_ref, gate_ref, gpost_ref, q_ref, k_ref, vt_ref, augq_ref, augk_ref,
                     wout_ref, o_ref, lhs_ref, stage_ref, mx_ref, alpha_ref, acc_ref, heads_ref):
    tq = x_ref.shape[0]
    n_pairs = q_ref.shape[1]
    qi = pl.program_id(1)
    lanes = lax.broadcasted_iota(jnp.int32, (tq, V7X_LANES), 1)
    half = lanes // FOX_HEAD_DIM
    causal_t = (lax.broadcasted_iota(jnp.int32, (tq, tq), 0)
                <= lax.broadcasted_iota(jnp.int32, (tq, tq), 1))
    augq = augq_ref[...]
    contract_last = (((1,), (1,)), ((), ()))
    zero16 = jnp.zeros((), BF16)
    ones_rows = jnp.ones((FOX_HEAD_DIM, tq), BF16)
    n_slabs = tq // V7X_SUBLANES

    def key_rows(kb):
        return pl.ds(pl.multiple_of(kb * tq, tq), tq)

    def over_rows(vec, n_rows):
        return jnp.concatenate([vec] * (n_rows // V7X_SUBLANES), axis=0)

    def stage(pairs):
        for p in pairs:
            q_pair = q_ref[0, p]
            for e in range(2):
                h = 2 * p + e
                own_aug = (lanes >= AUG_PER_HEAD * h) & (lanes < AUG_PER_HEAD * (h + 1))
                lhs_ref[h] = jnp.concatenate(
                    [jnp.where(half == e, q_pair, zero16), jnp.where(own_aug, augq, zero16)], axis=1)
                mx_ref[h] = jnp.full((V7X_SUBLANES, tq), -jnp.inf, F32)
                acc_ref[h] = jnp.zeros((V7X_LANES, tq), F32)

    def attend(pairs, kb, n_blocks, masked, slot):
        ones = jnp.concatenate([ones_rows] * n_blocks, axis=1)
        for p in pairs:
            keys = [jnp.concatenate([k_ref[0, p, key_rows(kb + j), :], augk_ref[key_rows(kb + j), :]], axis=1)
                    for j in range(n_blocks)]
            for e in range(2):
                h = 2 * p + e
                hh = h - 2 * pairs[0]
                slabs = []
                for j in range(n_blocks):
                    s = lax.dot_general(keys[j], lhs_ref[h], contract_last, preferred_element_type=F32)
                    if masked:
                        s = jnp.where(causal_t, s, -jnp.inf)
                    stage_ref[slot, hh, j] = s
                    slabs += [s[i * V7X_SUBLANES:(i + 1) * V7X_SUBLANES] for i in range(n_slabs)]
                while len(slabs) > 1:
                    slabs = [jnp.maximum(a, b) for a, b in zip(slabs[0::2], slabs[1::2])]
                m_old = mx_ref[h]
                m_new = jnp.maximum(m_old, jnp.broadcast_to(jnp.max(slabs[0], axis=0, keepdims=True),
                                                            (V7X_SUBLANES, tq)))
                mx_ref[h] = m_new
                alpha_ref[h] = jnp.exp2(m_old - m_new)
        for p in pairs:
            vt_pair = jnp.concatenate([vt_ref[0, p, kb + j] for j in range(n_blocks)], axis=1)
            for e in range(2):
                h = 2 * p + e
                hh = h - 2 * pairs[0]
                m_rows = over_rows(mx_ref[h], tq)
                prob = jnp.concatenate(
                    [jnp.exp2(stage_ref[slot, hh, j] - m_rows).astype(BF16) for j in range(n_blocks)], axis=0)
                vt_own = (jnp.concatenate([vt_pair[:FOX_HEAD_DIM], ones], axis=0) if e == 0 else
                          jnp.concatenate([ones, vt_pair[FOX_HEAD_DIM:]], axis=0))
                acc_ref[h] = (over_rows(alpha_ref[h], V7X_LANES) * acc_ref[h]
                              + jnp.dot(vt_own, prob, preferred_element_type=F32))

    def finalize(pairs):
        for p in pairs:
            a_even = acc_ref[2 * p]
            a_odd = acc_ref[2 * p + 1]
            numer = jnp.concatenate([a_even[:FOX_HEAD_DIM], a_odd[FOX_HEAD_DIM:]], axis=0)
            denom = jnp.concatenate([a_even[FOX_HEAD_DIM:], a_odd[:FOX_HEAD_DIM]], axis=0)
            heads_ref[p] = (numer / denom).T.astype(BF16)

    for g in range(n_pairs // group_pairs):
        pairs = range(g * group_pairs, (g + 1) * group_pairs)
        stage(pairs)
        attend(pairs, qi, 1, True, qi % 2)

        def body(i, c, pairs=pairs):
            attend(pairs, 2 * i, 2, False, i % 2)
            return c

        lax.fori_loop(0, qi // 2, body, 0)

        @pl.when(qi % 2 == 1)
        def _(pairs=pairs):
            attend(pairs, qi - 1, 1, False, qi % 2)

        finalize(pairs)

    o = jnp.concatenate([heads_ref[p] for p in range(n_pairs)], axis=1)
    y = jnp.dot(o.astype(F32), wout_ref[...], preferred_element_type=F32)
    o_ref[...] = _residual(x_ref[...], y, gate_ref, gpost_ref)


def _fox_attn(x2, bsz, seq, gate, g_post, q, k, vt, augq, augk, w_out_all, j):
    t, d = x2.shape
    n_pairs = d // V7X_LANES
    tq = min(ATTN_TILE, seq)
    nq = seq // tq
    group_pairs = min(ATTN_GROUP_PAIRS, n_pairs)
    assert n_pairs % group_pairs == 0
    row = pl.BlockSpec((tq, d), lambda b, i: (b * nq + i, 0))
    mod = pl.BlockSpec((1, 1, d), lambda b, i: (b, 0, 0))
    vec = pl.BlockSpec((1, d), lambda b, i: (0, 0))
    q_spec = pl.BlockSpec((1, n_pairs, tq, V7X_LANES), lambda b, i: (b, 0, i, 0))
    k_spec = pl.BlockSpec((1, n_pairs, seq, V7X_LANES), lambda b, i: (b, 0, 0, 0))
    vt_spec = pl.BlockSpec((1, n_pairs, nq, V7X_LANES, tq), lambda b, i: (b, 0, 0, 0, 0))
    augq_spec = pl.BlockSpec((tq, V7X_LANES), lambda b, i: (b * nq + i, 0))
    augk_spec = pl.BlockSpec((seq, V7X_LANES), lambda b, i: (b, 0))
    return pl.pallas_call(
        functools.partial(_fox_attn_kernel, group_pairs),
        grid=(bsz, nq),
        in_specs=[row, mod, vec, q_spec, k_spec, vt_spec, augq_spec, augk_spec,
                  _stacked_resident(w_out_all, (j,))],
        out_specs=row,
        out_shape=jax.ShapeDtypeStruct((t, d), F32),
        scratch_shapes=[pltpu.VMEM((2 * n_pairs, tq, 2 * V7X_LANES), BF16),
                        pltpu.VMEM((2, 2 * group_pairs, 2, tq, tq), F32),
                        pltpu.VMEM((2 * n_pairs, V7X_SUBLANES, tq), F32),
                        pltpu.VMEM((2 * n_pairs, V7X_SUBLANES, tq), F32),
                        pltpu.VMEM((2 * n_pairs, V7X_LANES, tq), F32),
                        pltpu.VMEM((n_pairs, tq, V7X_LANES), BF16)],
        compiler_params=_params(2),
        name="fox_attn",
    )(x2, gate, g_post, q, k, vt, augq, augk, w_out_all)


def kernel(x, c, w_cond, b_cond, norm_pre, norm_post, w_ffn_in, w_ffn_out, fox_w_in, fox_b_f, fox_w_out, sconv_w_in, sconv_conv_w, sconv_w_out, lru_w_in, lru_conv_w, lru_conv_b, lru_w_a, lru_b_a, lru_w_x, lru_b_x, lru_lambda, lru_w_out):
    bsz, seq, d = x.shape
    depth = w_cond.shape[0]
    assert d % V7X_LANES == 0 and V7X_LANES == 2 * FOX_HEAD_DIM
    assert all(seq % min(tile, seq) == 0 for tile in (ROW_TILE, PROJ_ROW_TILE, FFN_ROW_TILE, ATTN_TILE))

    mod = _modulation(c, w_cond, b_cond).reshape(depth, bsz, N_SUB, 3, 1, d)
    x2 = x.reshape(bsz * seq, d)
    w_ffn_in16 = w_ffn_in
    w_ffn_out16 = w_ffn_out

    for i in range(depth):
        def mods(s):
            return mod[i, :, s, 0], mod[i, :, s, 1], mod[i, :, s, 2]

        def gains(s):
            return norm_pre[i, s].reshape(1, d), norm_post[i, s].reshape(1, d)

        shift, scale, gate = mods(0)
        x2 = _ffn(x2, seq, shift, scale, gate, *gains(0), w_ffn_in16, w_ffn_out16, (i, 0))

        shift, scale, gate = mods(1)
        g_pre, g_post = gains(1)
        kind, j = i % N_MIXERS, i // N_MIXERS
        if kind == 0:
            q, k, vt, augq, augk = _fox_proj(x2, bsz, seq, shift, scale, g_pre, fox_w_in, j, fox_b_f[j])
            x2 = _fox_attn(x2, bsz, seq, gate, g_post, q, k, vt, augq, augk, fox_w_out, j)
        elif kind == 1:
            x2 = _sconv(x2, seq, shift, scale, gate, g_pre, g_post, sconv_w_in[j], sconv_conv_w[j],
                        sconv_w_out[j])
        else:
            x2 = _lru(x2, seq, shift, scale, gate, g_pre, g_post, lru_w_in[j], lru_conv_w[j],
                      lru_conv_b[j], lru_w_a[j], lru_b_a[j], lru_w_x[j], lru_b_x[j], lru_lambda[j],
                      lru_w_out[j])

        shift, scale, gate = mods(2)
        x2 = _ffn(x2, seq, shift, scale, gate, *gains(2), w_ffn_in16, w_ffn_out16, (i, 1))

    return x2.reshape(bsz, seq, d)
```

```python
import functools

import numpy as np
import jax
import jax.numpy as jnp
from jax import lax
from jax.experimental import pallas as pl
from jax.experimental.pallas import tpu as pltpu

N_SUB = 3
N_MIXERS = 3
FFN_RES_WEIGHT = 0.5
RMS_EPS = 1e-6
FOX_HEAD_DIM = 64
LRU_C = 8.0
LOG2E = 1.4426950408889634

V7X_LANES = 128
V7X_SUBLANES = 8
V7X_MXU_DIM = 256
V7X_VMEM_BYTES = 64 * 1024 * 1024
VMEM_LIMIT_BYTES = V7X_VMEM_BYTES - 8 * 1024 * 1024

ROW_TILE = 512
PROJ_ROW_TILE = 1024
ATTN_TILE = 256
FFN_CHUNK = V7X_MXU_DIM
FFN_ROW_TILE = 512
FFN_SUBTILE = 512
AUG_PER_HEAD = 6
ATTN_GROUP_PAIRS = 8
MOD_COL_TILE = 1536

BF16 = jnp.bfloat16
F32 = jnp.float32


def _params(n_axes):
    return pltpu.CompilerParams(dimension_semantics=("arbitrary",) * n_axes,
                                vmem_limit_bytes=VMEM_LIMIT_BYTES)


def _resident(shape):
    return pl.BlockSpec(shape, lambda *_: (0,) * len(shape), pipeline_mode=pl.Buffered(1))


def _rms(x):
    return x * lax.rsqrt(jnp.mean(x * x, axis=-1, keepdims=True) + RMS_EPS)


def _pre(x, gpre_ref, scale_ref, shift_ref):
    return _rms(x) * (gpre_ref[...] * (1.0 + scale_ref[0])) + shift_ref[0]


def _residual(x, y, gate_ref, gpost_ref, weight=1.0):
    return x + _rms(y) * ((weight * gate_ref[0]) * gpost_ref[...])


def _split3(x):
    hi = x.astype(BF16)
    r1 = x - hi.astype(F32)
    mid = r1.astype(BF16)
    lo = (r1 - mid.astype(F32)).astype(BF16)
    return hi, mid, lo


def _mod_kernel(c_ref, w_ref, b_ref, o_ref):
    c = c_ref[...]
    c_act = (c * jax.nn.sigmoid(c)).astype(BF16)
    o_ref[0] = jnp.dot(c_act, w_ref[0].astype(BF16), preferred_element_type=F32) + b_ref[0]


def _modulation(c, w_cond, b_cond):
    depth, d, n = w_cond.shape
    bsz = c.shape[0]
    tn = MOD_COL_TILE if n % MOD_COL_TILE == 0 else n
    return pl.pallas_call(
        _mod_kernel,
        grid=(depth, n // tn),
        in_specs=[
            pl.BlockSpec((bsz, d), lambda i, j: (0, 0)),
            pl.BlockSpec((1, d, tn), lambda i, j: (i, 0, j)),
            pl.BlockSpec((1, 1, tn), lambda i, j: (i, 0, j)),
        ],
        out_specs=pl.BlockSpec((1, bsz, tn), lambda i, j: (i, 0, j)),
        out_shape=jax.ShapeDtypeStruct((depth, bsz, n), F32),
        compiler_params=_params(2),
        name="adaln_mod",
    )(c, w_cond, b_cond.reshape(depth, 1, n))


def _ffn_kernel(x_ref, shift_ref, scale_ref, gate_ref, gpre_ref, gpost_ref, win_ref, wout_ref,
                o_ref, act_ref):
    f = wout_ref.shape[0]
    fc = FFN_CHUNK
    tm = x_ref.shape[0]
    for r in range(0, tm, FFN_SUBTILE):
        rows = slice(r, r + FFN_SUBTILE)
        x = x_ref[rows, :]
        h = _pre(x, gpre_ref, scale_ref, shift_ref)
        for c in range(f // fc):
            g = jnp.dot(h, win_ref[:, c * fc:(c + 1) * fc], preferred_element_type=F32)
            u = jnp.dot(h, win_ref[:, f + c * fc:f + (c + 1) * fc], preferred_element_type=F32)
            act_ref[rows, c * fc:(c + 1) * fc] = (g * jax.nn.sigmoid(g) * u).astype(BF16)
        y = jnp.dot(act_ref[rows, :].astype(F32), wout_ref[...], preferred_element_type=F32)
        o_ref[rows, :] = _residual(x, y, gate_ref, gpost_ref, FFN_RES_WEIGHT)


def _row_specs(tm, d, tiles_per_batch):
    row = pl.BlockSpec((tm, d), lambda i: (i, 0))
    mod = pl.BlockSpec((1, 1, d), lambda i: (i // tiles_per_batch, 0, 0))
    vec = pl.BlockSpec((1, d), lambda i: (0, 0))
    return row, mod, vec


def _stacked_resident(stacked, index):
    lead = len(index)
    shape = (None,) * lead + tuple(stacked.shape[lead:])
    return pl.BlockSpec(shape, lambda *_: tuple(index) + (0,) * (stacked.ndim - lead),
                        pipeline_mode=pl.Buffered(1))


def _ffn(x2, seq, shift, scale, gate, g_pre, g_post, w_in, w_out, index):
    t, d = x2.shape
    tm = min(FFN_ROW_TILE, seq)
    f = w_out.shape[-2]
    assert f % FFN_CHUNK == 0 and tm % FFN_SUBTILE == 0
    row, mod, vec = _row_specs(tm, d, seq // tm)
    return pl.pallas_call(
        _ffn_kernel,
        grid=(t // tm,),
        in_specs=[row, mod, mod, mod, vec, vec, _stacked_resident(w_in, index),
                  _stacked_resident(w_out, index)],
        out_specs=row,
        out_shape=jax.ShapeDtypeStruct((t, d), F32),
        scratch_shapes=[pltpu.VMEM((tm, f), BF16)],
        compiler_params=_params(1),
        name="ffn",
    )(x2, shift, scale, gate, g_pre, g_post, w_in, w_out)


def _causal_conv(u, hist_ref, w_ref):
    tm = u.shape[0]
    kw = w_ref.shape[0]
    hist_ref[V7X_SUBLANES:, :] = u
    out = w_ref[kw - 1:kw] * u
    for k in range(1, kw):
        out = out + w_ref[kw - 1 - k:kw - k] * hist_ref[V7X_SUBLANES - k:V7X_SUBLANES - k + tm, :]
    hist_ref[:V7X_SUBLANES, :] = u[tm - V7X_SUBLANES:]
    return out


def _sconv_kernel(tiles_per_batch, x_ref, shift_ref, scale_ref, gate_ref, gpre_ref, gpost_ref,
                  win_ref, convw_ref, wout_ref, o_ref, hist_ref):
    d = x_ref.shape[1]

    @pl.when(pl.program_id(0) % tiles_per_batch == 0)
    def _():
        hist_ref[:V7X_SUBLANES, :] = jnp.zeros((V7X_SUBLANES, d), F32)

    sub = hist_ref.shape[0] - V7X_SUBLANES
    for r in range(0, x_ref.shape[0], sub):
        rows = slice(r, r + sub)
        x = x_ref[rows, :]
        h = _pre(x, gpre_ref, scale_ref, shift_ref)
        c_gate = jnp.dot(h, win_ref[:, d:2 * d], preferred_element_type=F32)
        xv = jnp.dot(h, win_ref[:, 2 * d:], preferred_element_type=F32)
        u = c_gate * xv
        conv = _causal_conv(u, hist_ref, convw_ref)
        b_gate = jnp.dot(h, win_ref[:, :d], preferred_element_type=F32)
        y = jnp.dot(b_gate * conv, wout_ref[...], preferred_element_type=F32)
        o_ref[rows, :] = _residual(x, y, gate_ref, gpost_ref)


def _sconv(x2, seq, shift, scale, gate, g_pre, g_post, w_in, conv_w, w_out):
    t, d = x2.shape
    tm = min(PROJ_ROW_TILE, seq)
    sub = min(ROW_TILE, tm)
    assert tm % sub == 0
    row, mod, vec = _row_specs(tm, d, seq // tm)
    return pl.pallas_call(
        functools.partial(_sconv_kernel, seq // tm),
        grid=(t // tm,),
        in_specs=[row, mod, mod, mod, vec, vec, _resident(w_in.shape), _resident(conv_w.shape),
                  _resident(w_out.shape)],
        out_specs=row,
        out_shape=jax.ShapeDtypeStruct((t, d), F32),
        scratch_shapes=[pltpu.VMEM((V7X_SUBLANES + sub, d), F32)],
        compiler_params=_params(1),
        name="sconv_mixer",
    )(x2, shift, scale, gate, g_pre, g_post, w_in, conv_w, w_out)


def _linear_scan(a, b, h_prev):
    n, c = a.shape
    grouped = (n // V7X_SUBLANES, V7X_SUBLANES, c)
    sub = lax.broadcasted_iota(jnp.int32, grouped, 1)
    a = a.reshape(grouped)
    b = b.reshape(grouped)
    d = 1
    while d < V7X_SUBLANES:
        keep = sub >= d
        b = jnp.where(keep, b + a * pltpu.roll(b, d, axis=1), b)
        a = jnp.where(keep, a * pltpu.roll(a, d, axis=1), a)
        d *= 2
    a = a.reshape(n, c)
    b = b.reshape(n, c)
    groups = []
    for r in range(0, n, V7X_SUBLANES):
        h_group = a[r:r + V7X_SUBLANES] * h_prev + b[r:r + V7X_SUBLANES]
        groups.append(h_group)
        h_prev = h_group[V7X_SUBLANES - 1:]
    return jnp.concatenate(groups, axis=0)


def _lru_kernel(tiles_per_batch, x_ref, shift_ref, scale_ref, gate_ref, gpre_ref, gpost_ref,
                win_ref, convw_ref, convb_ref, wgate_ref, ba_ref, bx_ref, lam_ref, wout_ref,
                o_ref, xhist_ref, hcarry_ref):
    tm = x_ref.shape[0]
    w = wout_ref.shape[0]
    n_groups, gk, two_gn = wgate_ref.shape
    gn = two_gn // 2

    @pl.when(pl.program_id(0) % tiles_per_batch == 0)
    def _():
        xhist_ref[:V7X_SUBLANES, :] = jnp.zeros((V7X_SUBLANES, w), F32)
        hcarry_ref[...] = jnp.zeros(hcarry_ref.shape, F32)

    x = x_ref[...]
    h = _pre(x, gpre_ref, scale_ref, shift_ref).astype(BF16)
    xraw = jnp.dot(h, win_ref[:, w:], preferred_element_type=F32)
    xb = _causal_conv(xraw, xhist_ref, convw_ref) + convb_ref[...]

    xb16 = xb.astype(BF16)
    r_parts, i_parts = [], []
    for g in range(n_groups):
        ri = jnp.dot(xb16[:, g * gk:(g + 1) * gk], wgate_ref[g], preferred_element_type=F32)
        r_parts.append(ri[:, :gn])
        i_parts.append(ri[:, gn:])
    r = jax.nn.sigmoid(jnp.concatenate(r_parts, axis=1) + ba_ref[...])
    i = jax.nn.sigmoid(jnp.concatenate(i_parts, axis=1) + bx_ref[...])

    neg_lam = -lam_ref[...]
    softplus = jnp.maximum(neg_lam, 0.0) + jnp.log1p(jnp.exp(-jnp.abs(neg_lam)))
    log_a = (-LRU_C * softplus) * r
    a = jnp.exp(log_a)
    mult = jnp.sqrt(-jnp.tanh(log_a) * (1.0 + a * a))
    hs = _linear_scan(a, mult * (i * xb), hcarry_ref[V7X_SUBLANES - 1:])
    hcarry_ref[...] = hs[tm - V7X_SUBLANES:]

    gate_br = jnp.dot(h, win_ref[:, :w], preferred_element_type=F32)
    y = jnp.dot((hs * jax.nn.gelu(gate_br)).astype(BF16), wout_ref[...], preferred_element_type=F32)
    o_ref[...] = _residual(x, y, gate_ref, gpost_ref)


def _lru_gate_weights(w_a, w_x):
    nb, bd, _ = w_a.shape
    per = V7X_MXU_DIM // bd
    assert nb % per == 0

    def dense(wb):
        wg = wb.reshape(nb // per, per, bd, bd)
        eye = jnp.eye(per, dtype=wb.dtype)
        return jnp.einsum("gpij,pq->gpiqj", wg, eye).reshape(nb // per, per * bd, per * bd)

    return jnp.concatenate([dense(w_a), dense(w_x)], axis=-1).astype(BF16)


def _lru(x2, seq, shift, scale, gate, g_pre, g_post, w_in, conv_w, conv_b, w_a, b_a, w_x, b_x, lam,
         w_out):
    t, d = x2.shape
    w = w_out.shape[0]
    tm = min(ROW_TILE, seq)
    row, mod, vec = _row_specs(tm, d, seq // tm)
    wvec = pl.BlockSpec((1, w), lambda i: (0, 0))
    w_gate = _lru_gate_weights(w_a, w_x)
    return pl.pallas_call(
        functools.partial(_lru_kernel, seq // tm),
        grid=(t // tm,),
        in_specs=[row, mod, mod, mod, vec, vec, _resident(w_in.shape), _resident(conv_w.shape), wvec,
                  _resident(w_gate.shape), wvec, wvec, wvec, _resident(w_out.shape)],
        out_specs=row,
        out_shape=jax.ShapeDtypeStruct((t, d), F32),
        scratch_shapes=[pltpu.VMEM((V7X_SUBLANES + tm, w), F32), pltpu.VMEM((V7X_SUBLANES, w), F32)],
        compiler_params=_params(1),
        name="lru_mixer",
    )(x2, shift, scale, gate, g_pre, g_post, w_in.astype(BF16), conv_w, conv_b.reshape(1, w), w_gate,
      b_a.reshape(1, w), b_x.reshape(1, w), lam.reshape(1, w), w_out.astype(BF16))


def _fox_proj_kernel(tiles_per_batch, n_heads, x_ref, shift_ref, scale_ref, gpre_ref, win_ref, wvt_ref, wf_ref,
                     bf_ref, tri_ref, pqk_ref, oneq_ref, onek_ref,
                     q_ref, k_ref, vt_ref, augq_ref, augk_ref, carry_ref):
    d = x_ref.shape[1]
    n_pairs = q_ref.shape[1]
    tk = vt_ref.shape[4]

    @pl.when(pl.program_id(0) % tiles_per_batch == 0)
    def _():
        carry_ref[...] = jnp.zeros(carry_ref.shape, F32)

    sub = tri_ref.shape[0]
    for r in range(0, x_ref.shape[0], sub):
        rows = slice(r, r + sub)
        h = _pre(x_ref[rows, :], gpre_ref, scale_ref, shift_ref)
        q = jnp.dot(h, win_ref[:, :d], preferred_element_type=F32) * (FOX_HEAD_DIM ** -0.5 * LOG2E)
        k = jnp.dot(h, win_ref[:, d:2 * d], preferred_element_type=F32)
        vt = lax.dot_general(wvt_ref[...], h, (((1,), (1,)), ((), ())), preferred_element_type=F32)
        for p in range(n_pairs):
            sl = slice(p * V7X_LANES, (p + 1) * V7X_LANES)
            q_ref[0, p, rows, :] = q[:, sl].astype(BF16)
            k_ref[0, p, rows, :] = k[:, sl].astype(BF16)
            for j in range(sub // tk):
                vt_ref[0, p, r // tk + j] = vt[sl, j * tk:(j + 1) * tk].astype(BF16)

        f_logit = jnp.dot(h, wf_ref[...], preferred_element_type=F32) + bf_ref[...]
        log_f = jnp.minimum(f_logit, 0.0) - jnp.log1p(jnp.exp(-jnp.abs(f_logit)))
        lanes = lax.broadcasted_iota(jnp.int32, log_f.shape, 1)
        log_f = jnp.where(lanes < n_heads, log_f, 0.0)
        sums = jnp.dot(tri_ref[...], jnp.concatenate(_split3(log_f), axis=1), preferred_element_type=F32)
        cum = (carry_ref[V7X_SUBLANES - 1:] + sums[:, :V7X_LANES] + sums[:, V7X_LANES:2 * V7X_LANES]
               + sums[:, 2 * V7X_LANES:])
        carry_ref[...] = cum[sub - V7X_SUBLANES:]

        parts = jnp.concatenate(_split3(cum * LOG2E), axis=1)
        aug = jnp.dot(parts, pqk_ref[...], preferred_element_type=F32)
        augq_ref[rows, :] = (aug[:, :V7X_LANES] + oneq_ref[...]).astype(BF16)
        augk_ref[rows, :] = (aug[:, V7X_LANES:] + onek_ref[...]).astype(BF16)


def _aug_constants(n_heads):
    assert n_heads * AUG_PER_HEAD <= V7X_LANES
    pq = np.zeros((3 * V7X_LANES, V7X_LANES), np.float32)
    pk = np.zeros((3 * V7X_LANES, V7X_LANES), np.float32)
    oneq = np.zeros((1, V7X_LANES), np.float32)
    onek = np.zeros((1, V7X_LANES), np.float32)
    for h in range(n_heads):
        for j in range(3):
            pq[j * V7X_LANES + h, AUG_PER_HEAD * h + j] = 1.0
            pk[j * V7X_LANES + h, AUG_PER_HEAD * h + 3 + j] = -1.0
            oneq[0, AUG_PER_HEAD * h + 3 + j] = 1.0
            onek[0, AUG_PER_HEAD * h + j] = 1.0
    return jnp.asarray(np.concatenate([pq, pk], axis=1), BF16), jnp.asarray(oneq), jnp.asarray(onek)


def _fox_proj(x2, bsz, seq, shift, scale, g_pre, w_in_all, j, b_f):
    t, d = x2.shape
    n_heads = d // FOX_HEAD_DIM
    n_pairs = d // V7X_LANES
    tm = min(PROJ_ROW_TILE, seq)
    sub = min(ROW_TILE, tm)
    tiles_per_batch = seq // tm
    row, mod, vec = _row_specs(tm, d, tiles_per_batch)
    tk = min(ATTN_TILE, seq)
    assert tm % sub == 0 and sub % tk == 0
    w_qk = w_in_all[j, :, :2 * d]
    w_vt = w_in_all[j, :, 2 * d:3 * d].T
    w_f = jnp.pad(w_in_all[j, :, 3 * d:], ((0, 0), (0, V7X_LANES - n_heads)))
    b_f = jnp.pad(b_f, (0, V7X_LANES - n_heads)).reshape(1, V7X_LANES)
    tri = jnp.asarray(np.tril(np.ones((sub, sub), np.float32)), BF16)
    pqk, oneq, onek = _aug_constants(n_heads)
    pair_spec = pl.BlockSpec((1, n_pairs, tm, V7X_LANES),
                             lambda i: (i // tiles_per_batch, 0, i % tiles_per_batch, 0))
    vt_spec = pl.BlockSpec((1, n_pairs, tm // tk, V7X_LANES, tk),
                           lambda i: (i // tiles_per_batch, 0, i % tiles_per_batch, 0, 0))
    aug_spec = pl.BlockSpec((tm, V7X_LANES), lambda i: (i, 0))
    pair_shape = jax.ShapeDtypeStruct((bsz, n_pairs, seq, V7X_LANES), BF16)
    vt_shape = jax.ShapeDtypeStruct((bsz, n_pairs, seq // tk, V7X_LANES, tk), BF16)
    aug_shape = jax.ShapeDtypeStruct((t, V7X_LANES), BF16)
    return pl.pallas_call(
        functools.partial(_fox_proj_kernel, tiles_per_batch, n_heads),
        grid=(t // tm,),
        in_specs=[row, mod, mod, vec, _resident(w_qk.shape), _resident(w_vt.shape), _resident(w_f.shape),
                  _resident(b_f.shape), _resident(tri.shape), _resident(pqk.shape),
                  _resident(oneq.shape), _resident(onek.shape)],
        out_specs=[pair_spec, pair_spec, vt_spec, aug_spec, aug_spec],
        out_shape=[pair_shape, pair_shape, vt_shape, aug_shape, aug_shape],
        scratch_shapes=[pltpu.VMEM((V7X_SUBLANES, V7X_LANES), F32)],
        compiler_params=_params(1),
        name="fox_proj",
    )(x2, shift, scale, g_pre, w_qk, w_vt, w_f, b_f, tri, pqk, oneq, onek)


def _fox_attn_kernel(group_pairs, x_ref, gate_ref, gpost_ref, q_ref, k_ref, vt_ref, augq_ref, augk_ref,
                     wout_ref, o_ref, lhs_ref, stage_ref, mx_ref, alpha_ref, acc_ref, heads_ref):
    tq = x_ref.shape[0]
    n_pairs = q_ref.shape[1]
    qi = pl.program_id(1)
    lanes = lax.broadcasted_iota(jnp.int32, (tq, V7X_LANES), 1)
    half = lanes // FOX_HEAD_DIM
    causal_t = (lax.broadcasted_iota(jnp.int32, (tq, tq), 0)
                <= lax.broadcasted_iota(jnp.int32, (tq, tq), 1))
    augq = augq_ref[...]
    contract_last = (((1,), (1,)), ((), ()))
    zero16 = jnp.zeros((), BF16)
    ones_rows = jnp.ones((FOX_HEAD_DIM, tq), BF16)
    n_slabs = tq // V7X_SUBLANES

    def key_rows(kb):
        return pl.ds(pl.multiple_of(kb * tq, tq), tq)

    def over_rows(vec, n_rows):
        return jnp.concatenate([vec] * (n_rows // V7X_SUBLANES), axis=0)

    def stage(pairs):
        for p in pairs:
            q_pair = q_ref[0, p]
            for e in range(2):
                h = 2 * p + e
                own_aug = (lanes >= AUG_PER_HEAD * h) & (lanes < AUG_PER_HEAD * (h + 1))
                lhs_ref[h] = jnp.concatenate(
                    [jnp.where(half == e, q_pair, zero16), jnp.where(own_aug, augq, zero16)], axis=1)
                mx_ref[h] = jnp.full((V7X_SUBLANES, tq), -jnp.inf, F32)
                acc_ref[h] = jnp.zeros((V7X_LANES, tq), F32)

    def attend(pairs, kb, n_blocks, masked, slot):
        ones = jnp.concatenate([ones_rows] * n_blocks, axis=1)
        for p in pairs:
            keys = [jnp.concatenate([k_ref[0, p, key_rows(kb + j), :], augk_ref[key_rows(kb + j), :]], axis=1)
                    for j in range(n_blocks)]
            for e in range(2):
                h = 2 * p + e
                hh = h - 2 * pairs[0]
                slabs = []
                for j in range(n_blocks):
                    s = lax.dot_general(keys[j], lhs_ref[h], contract_last, preferred_element_type=F32)
                    if masked:
                        s = jnp.where(causal_t, s, -jnp.inf)
                    stage_ref[slot, hh, j] = s
                    slabs += [s[i * V7X_SUBLANES:(i + 1) * V7X_SUBLANES] for i in range(n_slabs)]
                while len(slabs) > 1:
                    slabs = [jnp.maximum(a, b) for a, b in zip(slabs[0::2], slabs[1::2])]
                m_old = mx_ref[h]
                m_new = jnp.maximum(m_old, jnp.broadcast_to(jnp.max(slabs[0], axis=0, keepdims=True),
                                                            (V7X_SUBLANES, tq)))
                mx_ref[h] = m_new
                alpha_ref[h] = jnp.exp2(m_old - m_new)
        for p in pairs:
            vt_pair = jnp.concatenate([vt_ref[0, p, kb + j] for j in range(n_blocks)], axis=1)
            for e in range(2):
                h = 2 * p + e
                hh = h - 2 * pairs[0]
                m_rows = over_rows(mx_ref[h], tq)
                prob = jnp.concatenate(
                    [jnp.exp2(stage_ref[slot, hh, j] - m_rows).astype(BF16) for j in range(n_blocks)], axis=0)
                vt_own = (jnp.concatenate([vt_pair[:FOX_HEAD_DIM], ones], axis=0) if e == 0 else
                          jnp.concatenate([ones, vt_pair[FOX_HEAD_DIM:]], axis=0))
                acc_ref[h] = (over_rows(alpha_ref[h], V7X_LANES) * acc_ref[h]
                              + jnp.dot(vt_own, prob, preferred_element_type=F32))

    def finalize(pairs):
        for p in pairs:
            a_even = acc_ref[2 * p]
            a_odd = acc_ref[2 * p + 1]
            numer = jnp.concatenate([a_even[:FOX_HEAD_DIM], a_odd[FOX_HEAD_DIM:]], axis=0)
            denom = jnp.concatenate([a_even[FOX_HEAD_DIM:], a_odd[:FOX_HEAD_DIM]], axis=0)
            heads_ref[p] = (numer / denom).T.astype(BF16)

    for g in range(n_pairs // group_pairs):
        pairs = range(g * group_pairs, (g + 1) * group_pairs)
        stage(pairs)
        attend(pairs, qi, 1, True, qi % 2)

        def body(i, c, pairs=pairs):
            attend(pairs, 2 * i, 2, False, i % 2)
            return c

        lax.fori_loop(0, qi // 2, body, 0)

        @pl.when(qi % 2 == 1)
        def _(pairs=pairs):
            attend(pairs, qi - 1, 1, False, qi % 2)

        finalize(pairs)

    o = jnp.concatenate([heads_ref[p] for p in range(n_pairs)], axis=1)
    y = jnp.dot(o.astype(F32), wout_ref[...], preferred_element_type=F32)
    o_ref[...] = _residual(x_ref[...], y, gate_ref, gpost_ref)


def _fox_attn(x2, bsz, seq, gate, g_post, q, k, vt, augq, augk, w_out_all, j):
    t, d = x2.shape
    n_pairs = d // V7X_LANES
    tq = min(ATTN_TILE, seq)
    nq = seq // tq
    group_pairs = min(ATTN_GROUP_PAIRS, n_pairs)
    assert n_pairs % group_pairs == 0
    row = pl.BlockSpec((tq, d), lambda b, i: (b * nq + i, 0))
    mod = pl.BlockSpec((1, 1, d), lambda b, i: (b, 0, 0))
    vec = pl.BlockSpec((1, d), lambda b, i: (0, 0))
    q_spec = pl.BlockSpec((1, n_pairs, tq, V7X_LANES), lambda b, i: (b, 0, i, 0))
    k_spec = pl.BlockSpec((1, n_pairs, seq, V7X_LANES), lambda b, i: (b, 0, 0, 0))
    vt_spec = pl.BlockSpec((1, n_pairs, nq, V7X_LANES, tq), lambda b, i: (b, 0, 0, 0, 0))
    augq_spec = pl.BlockSpec((tq, V7X_LANES), lambda b, i: (b * nq + i, 0))
    augk_spec = pl.BlockSpec((seq, V7X_LANES), lambda b, i: (b, 0))
    return pl.pallas_call(
        functools.partial(_fox_attn_kernel, group_pairs),
        grid=(bsz, nq),
        in_specs=[row, mod, vec, q_spec, k_spec, vt_spec, augq_spec, augk_spec,
                  _stacked_resident(w_out_all, (j,))],
        out_specs=row,
        out_shape=jax.ShapeDtypeStruct((t, d), F32),
        scratch_shapes=[pltpu.VMEM((2 * n_pairs, tq, 2 * V7X_LANES), BF16),
                        pltpu.VMEM((2, 2 * group_pairs, 2, tq, tq), F32),
                        pltpu.VMEM((2 * n_pairs, V7X_SUBLANES, tq), F32),
                        pltpu.VMEM((2 * n_pairs, V7X_SUBLANES, tq), F32),
                        pltpu.VMEM((2 * n_pairs, V7X_LANES, tq), F32),
                        pltpu.VMEM((n_pairs, tq, V7X_LANES), BF16)],
        compiler_params=_params(2),
        name="fox_attn",
    )(x2, gate, g_post, q, k, vt, augq, augk, w_out_all)


def kernel(x, c, w_cond, b_cond, norm_pre, norm_post, w_ffn_in, w_ffn_out, fox_w_in, fox_b_f, fox_w_out, sconv_w_in, sconv_conv_w, sconv_w_out, lru_w_in, lru_conv_w, lru_conv_b, lru_w_a, lru_b_a, lru_w_x, lru_b_x, lru_lambda, lru_w_out):
    bsz, seq, d = x.shape
    depth = w_cond.shape[0]
    assert d % V7X_LANES == 0 and V7X_LANES == 2 * FOX_HEAD_DIM
    assert all(seq % min(tile, seq) == 0 for tile in (ROW_TILE, PROJ_ROW_TILE, FFN_ROW_TILE, ATTN_TILE))

    mod = _modulation(c, w_cond, b_cond).reshape(depth, bsz, N_SUB, 3, 1, d)
    x2 = x.reshape(bsz * seq, d)
    w_ffn_in16 = w_ffn_in
    w_ffn_out16 = w_ffn_out

    for i in range(depth):
        def mods(s):
            return mod[i, :, s, 0], mod[i, :, s, 1], mod[i, :, s, 2]

        def gains(s):
            return norm_pre[i, s].reshape(1, d), norm_post[i, s].reshape(1, d)

        shift, scale, gate = mods(0)
        x2 = _ffn(x2, seq, shift, scale, gate, *gains(0), w_ffn_in16, w_ffn_out16, (i, 0))

        shift, scale, gate = mods(1)
        g_pre, g_post = gains(1)
        kind, j = i % N_MIXERS, i // N_MIXERS
        if kind == 0:
            q, k, vt, augq, augk = _fox_proj(x2, bsz, seq, shift, scale, g_pre, fox_w_in, j, fox_b_f[j])
            x2 = _fox_attn(x2, bsz, seq, gate, g_post, q, k, vt, augq, augk, fox_w_out, j)
        elif kind == 1:
            x2 = _sconv(x2, seq, shift, scale, gate, g_pre, g_post, sconv_w_in[j], sconv_conv_w[j],
                        sconv_w_out[j])
        else:
            x2 = _lru(x2, seq, shift, scale, gate, g_pre, g_post, lru_w_in[j], lru_conv_w[j],
                      lru_conv_b[j], lru_w_a[j], lru_b_a[j], lru_w_x[j], lru_b_x[j], lru_lambda[j],
                      lru_w_out[j])

        shift, scale, gate = mods(2)
        x2 = _ffn(x2, seq, shift, scale, gate, *gains(2), w_ffn_in16, w_ffn_out16, (i, 1))

    return x2.reshape(bsz, seq, d)
```

```python
import functools

import numpy as np
import jax
import jax.numpy as jnp
from jax import lax
from jax.experimental import pallas as pl
from jax.experimental.pallas import tpu as pltpu

N_SUB = 3
N_MIXERS = 3
FFN_RES_WEIGHT = 0.5
RMS_EPS = 1e-6
FOX_HEAD_DIM = 64
LRU_C = 8.0
LOG2E = 1.4426950408889634

V7X_LANES = 128
V7X_SUBLANES = 8
V7X_MXU_DIM = 256
V7X_VMEM_BYTES = 64 * 1024 * 1024
VMEM_LIMIT_BYTES = V7X_VMEM_BYTES - 8 * 1024 * 1024

ROW_TILE = 512
PROJ_ROW_TILE = 1024
ATTN_TILE = 256
FFN_CHUNK = V7X_MXU_DIM
FFN_ROW_TILE = 512
AUG_PER_HEAD = 6
ATTN_GROUP_PAIRS = 8
MOD_COL_TILE = 3072

BF16 = jnp.bfloat16
F32 = jnp.float32


def _params(n_axes):
    return pltpu.CompilerParams(dimension_semantics=("arbitrary",) * n_axes,
                                vmem_limit_bytes=VMEM_LIMIT_BYTES)


def _resident(shape):
    return pl.BlockSpec(shape, lambda *_: (0,) * len(shape), pipeline_mode=pl.Buffered(1))


def _rms(x):
    return x * lax.rsqrt(jnp.mean(x * x, axis=-1, keepdims=True) + RMS_EPS)


def _pre(x, gpre_ref, scale_ref, shift_ref):
    return _rms(x) * (gpre_ref[...] * (1.0 + scale_ref[0])) + shift_ref[0]


def _residual(x, y, gate_ref, gpost_ref, weight=1.0):
    return x + _rms(y) * ((weight * gate_ref[0]) * gpost_ref[...])


def _split3(x):
    hi = x.astype(BF16)
    r1 = x - hi.astype(F32)
    mid = r1.astype(BF16)
    lo = (r1 - mid.astype(F32)).astype(BF16)
    return hi, mid, lo


def _mod_kernel(c_ref, w_ref, b_ref, o_ref):
    c = c_ref[...]
    c_act = (c * jax.nn.sigmoid(c)).astype(BF16)
    o_ref[0] = jnp.dot(c_act, w_ref[0].astype(BF16), preferred_element_type=F32) + b_ref[0]


def _modulation(c, w_cond, b_cond):
    depth, d, n = w_cond.shape
    bsz = c.shape[0]
    tn = MOD_COL_TILE if n % MOD_COL_TILE == 0 else n
    return pl.pallas_call(
        _mod_kernel,
        grid=(depth, n // tn),
        in_specs=[
            pl.BlockSpec((bsz, d), lambda i, j: (0, 0)),
            pl.BlockSpec((1, d, tn), lambda i, j: (i, 0, j)),
            pl.BlockSpec((1, 1, tn), lambda i, j: (i, 0, j)),
        ],
        out_specs=pl.BlockSpec((1, bsz, tn), lambda i, j: (i, 0, j)),
        out_shape=jax.ShapeDtypeStruct((depth, bsz, n), F32),
        compiler_params=_params(2),
        name="adaln_mod",
    )(c, w_cond, b_cond.reshape(depth, 1, n))


def _ffn_kernel(x_ref, shift_ref, scale_ref, gate_ref, gpre_ref, gpost_ref, win_ref, wout_ref,
                o_ref, act_ref):
    f = wout_ref.shape[0]
    fc = FFN_CHUNK
    x = x_ref[...]
    h = _pre(x, gpre_ref, scale_ref, shift_ref)
    for c in range(f // fc):
        g = jnp.dot(h, win_ref[:, c * fc:(c + 1) * fc], preferred_element_type=F32)
        u = jnp.dot(h, win_ref[:, f + c * fc:f + (c + 1) * fc], preferred_element_type=F32)
        act_ref[:, c * fc:(c + 1) * fc] = (g * jax.nn.sigmoid(g) * u).astype(BF16)
    y = jnp.dot(act_ref[...].astype(F32), wout_ref[...], preferred_element_type=F32)
    o_ref[...] = _residual(x, y, gate_ref, gpost_ref, FFN_RES_WEIGHT)


def _row_specs(tm, d, tiles_per_batch):
    row = pl.BlockSpec((tm, d), lambda i: (i, 0))
    mod = pl.BlockSpec((1, 1, d), lambda i: (i // tiles_per_batch, 0, 0))
    vec = pl.BlockSpec((1, d), lambda i: (0, 0))
    return row, mod, vec


def _stacked_resident(stacked, index):
    lead = len(index)
    shape = (None,) * lead + tuple(stacked.shape[lead:])
    return pl.BlockSpec(shape, lambda *_: tuple(index) + (0,) * (stacked.ndim - lead),
                        pipeline_mode=pl.Buffered(1))


def _ffn(x2, seq, shift, scale, gate, g_pre, g_post, w_in, w_out, index):
    t, d = x2.shape
    tm = min(FFN_ROW_TILE, seq)
    f = w_out.shape[-2]
    assert f % FFN_CHUNK == 0
    row, mod, vec = _row_specs(tm, d, seq // tm)
    return pl.pallas_call(
        _ffn_kernel,
        grid=(t // tm,),
        in_specs=[row, mod, mod, mod, vec, vec, _stacked_resident(w_in, index),
                  _stacked_resident(w_out, index)],
        out_specs=row,
        out_shape=jax.ShapeDtypeStruct((t, d), F32),
        scratch_shapes=[pltpu.VMEM((tm, f), BF16)],
        compiler_params=_params(1),
        name="ffn",
    )(x2, shift, scale, gate, g_pre, g_post, w_in, w_out)


def _causal_conv(u, hist_ref, w_ref):
    tm = u.shape[0]
    kw = w_ref.shape[0]
    hist_ref[V7X_SUBLANES:, :] = u
    out = w_ref[kw - 1:kw] * u
    for k in range(1, kw):
        out = out + w_ref[kw - 1 - k:kw - k] * hist_ref[V7X_SUBLANES - k:V7X_SUBLANES - k + tm, :]
    hist_ref[:V7X_SUBLANES, :] = u[tm - V7X_SUBLANES:]
    return out


def _sconv_kernel(tiles_per_batch, x_ref, shift_ref, scale_ref, gate_ref, gpre_ref, gpost_ref,
                  win_ref, convw_ref, wout_ref, o_ref, hist_ref):
    d = x_ref.shape[1]

    @pl.when(pl.program_id(0) % tiles_per_batch == 0)
    def _():
        hist_ref[:V7X_SUBLANES, :] = jnp.zeros((V7X_SUBLANES, d), F32)

    sub = hist_ref.shape[0] - V7X_SUBLANES
    for r in range(0, x_ref.shape[0], sub):
        rows = slice(r, r + sub)
        x = x_ref[rows, :]
        h = _pre(x, gpre_ref, scale_ref, shift_ref)
        c_gate = jnp.dot(h, win_ref[:, d:2 * d], preferred_element_type=F32)
        xv = jnp.dot(h, win_ref[:, 2 * d:], preferred_element_type=F32)
        u = c_gate * xv
        conv = _causal_conv(u, hist_ref, convw_ref)
        b_gate = jnp.dot(h, win_ref[:, :d], preferred_element_type=F32)
        y = jnp.dot(b_gate * conv, wout_ref[...], preferred_element_type=F32)
        o_ref[rows, :] = _residual(x, y, gate_ref, gpost_ref)


def _sconv(x2, seq, shift, scale, gate, g_pre, g_post, w_in, conv_w, w_out):
    t, d = x2.shape
    tm = min(PROJ_ROW_TILE, seq)
    sub = min(ROW_TILE, tm)
    assert tm % sub == 0
    row, mod, vec = _row_specs(tm, d, seq // tm)
    return pl.pallas_call(
        functools.partial(_sconv_kernel, seq // tm),
        grid=(t // tm,),
        in_specs=[row, mod, mod, mod, vec, vec, _resident(w_in.shape), _resident(conv_w.shape),
                  _resident(w_out.shape)],
        out_specs=row,
        out_shape=jax.ShapeDtypeStruct((t, d), F32),
        scratch_shapes=[pltpu.VMEM((V7X_SUBLANES + sub, d), F32)],
        compiler_params=_params(1),
        name="sconv_mixer",
    )(x2, shift, scale, gate, g_pre, g_post, w_in, conv_w, w_out)


def _linear_scan(a, b, h_prev):
    n, c = a.shape
    grouped = (n // V7X_SUBLANES, V7X_SUBLANES, c)
    sub = lax.broadcasted_iota(jnp.int32, grouped, 1)
    a = a.reshape(grouped)
    b = b.reshape(grouped)
    d = 1
    while d < V7X_SUBLANES:
        keep = sub >= d
        b = jnp.where(keep, b + a * pltpu.roll(b, d, axis=1), b)
        a = jnp.where(keep, a * pltpu.roll(a, d, axis=1), a)
        d *= 2
    a = a.reshape(n, c)
    b = b.reshape(n, c)
    groups = []
    for r in range(0, n, V7X_SUBLANES):
        h_group = a[r:r + V7X_SUBLANES] * h_prev + b[r:r + V7X_SUBLANES]
        groups.append(h_group)
        h_prev = h_group[V7X_SUBLANES - 1:]
    return jnp.concatenate(groups, axis=0)


def _lru_kernel(tiles_per_batch, x_ref, shift_ref, scale_ref, gate_ref, gpre_ref, gpost_ref,
                win_ref, convw_ref, convb_ref, wgate_ref, ba_ref, bx_ref, lam_ref, wout_ref,
                o_ref, xhist_ref, hcarry_ref):
    tm = x_ref.shape[0]
    w = wout_ref.shape[0]
    n_groups, gk, two_gn = wgate_ref.shape
    gn = two_gn // 2

    @pl.when(pl.program_id(0) % tiles_per_batch == 0)
    def _():
        xhist_ref[:V7X_SUBLANES, :] = jnp.zeros((V7X_SUBLANES, w), F32)
        hcarry_ref[...] = jnp.zeros(hcarry_ref.shape, F32)

    x = x_ref[...]
    h = _pre(x, gpre_ref, scale_ref, shift_ref).astype(BF16)
    xraw = jnp.dot(h, win_ref[:, w:], preferred_element_type=F32)
    xb = _causal_conv(xraw, xhist_ref, convw_ref) + convb_ref[...]

    xb16 = xb.astype(BF16)
    r_parts, i_parts = [], []
    for g in range(n_groups):
        ri = jnp.dot(xb16[:, g * gk:(g + 1) * gk], wgate_ref[g], preferred_element_type=F32)
        r_parts.append(ri[:, :gn])
        i_parts.append(ri[:, gn:])
    r = jax.nn.sigmoid(jnp.concatenate(r_parts, axis=1) + ba_ref[...])
    i = jax.nn.sigmoid(jnp.concatenate(i_parts, axis=1) + bx_ref[...])

    neg_lam = -lam_ref[...]
    softplus = jnp.maximum(neg_lam, 0.0) + jnp.log1p(jnp.exp(-jnp.abs(neg_lam)))
    log_a = (-LRU_C * softplus) * r
    a = jnp.exp(log_a)
    mult = jnp.sqrt(-jnp.tanh(log_a) * (1.0 + a * a))
    hs = _linear_scan(a, mult * (i * xb), hcarry_ref[V7X_SUBLANES - 1:])
    hcarry_ref[...] = hs[tm - V7X_SUBLANES:]

    gate_br = jnp.dot(h, win_ref[:, :w], preferred_element_type=F32)
    y = jnp.dot((hs * jax.nn.gelu(gate_br)).astype(BF16), wout_ref[...], preferred_element_type=F32)
    o_ref[...] = _residual(x, y, gate_ref, gpost_ref)


def _lru_gate_weights(w_a, w_x):
    nb, bd, _ = w_a.shape
    per = V7X_MXU_DIM // bd
    assert nb % per == 0

    def dense(wb):
        wg = wb.reshape(nb // per, per, bd, bd)
        eye = jnp.eye(per, dtype=wb.dtype)
        return jnp.einsum("gpij,pq->gpiqj", wg, eye).reshape(nb // per, per * bd, per * bd)

    return jnp.concatenate([dense(w_a), dense(w_x)], axis=-1).astype(BF16)


def _lru(x2, seq, shift, scale, gate, g_pre, g_post, w_in, conv_w, conv_b, w_a, b_a, w_x, b_x, lam,
         w_out):
    t, d = x2.shape
    w = w_out.shape[0]
    tm = min(ROW_TILE, seq)
    row, mod, vec = _row_specs(tm, d, seq // tm)
    wvec = pl.BlockSpec((1, w), lambda i: (0, 0))
    w_gate = _lru_gate_weights(w_a, w_x)
    return pl.pallas_call(
        functools.partial(_lru_kernel, seq // tm),
        grid=(t // tm,),
        in_specs=[row, mod, mod, mod, vec, vec, _resident(w_in.shape), _resident(conv_w.shape), wvec,
                  _resident(w_gate.shape), wvec, wvec, wvec, _resident(w_out.shape)],
        out_specs=row,
        out_shape=jax.ShapeDtypeStruct((t, d), F32),
        scratch_shapes=[pltpu.VMEM((V7X_SUBLANES + tm, w), F32), pltpu.VMEM((V7X_SUBLANES, w), F32)],
        compiler_params=_params(1),
        name="lru_mixer",
    )(x2, shift, scale, gate, g_pre, g_post, w_in.astype(BF16), conv_w, conv_b.reshape(1, w), w_gate,
      b_a.reshape(1, w), b_x.reshape(1, w), lam.reshape(1, w), w_out.astype(BF16))


def _fox_proj_kernel(tiles_per_batch, n_heads, x_ref, shift_ref, scale_ref, gpre_ref, win_ref, wvt_ref, wf_ref,
                     bf_ref, tri_ref, pqk_ref, oneq_ref, onek_ref,
                     q_ref, k_ref, vt_ref, augq_ref, augk_ref, carry_ref):
    d = x_ref.shape[1]
    n_pairs = q_ref.shape[1]
    tk = vt_ref.shape[4]

    @pl.when(pl.program_id(0) % tiles_per_batch == 0)
    def _():
        carry_ref[...] = jnp.zeros(carry_ref.shape, F32)

    sub = tri_ref.shape[0]
    for r in range(0, x_ref.shape[0], sub):
        rows = slice(r, r + sub)
        h = _pre(x_ref[rows, :], gpre_ref, scale_ref, shift_ref)
        q = jnp.dot(h, win_ref[:, :d], preferred_element_type=F32) * (FOX_HEAD_DIM ** -0.5 * LOG2E)
        k = jnp.dot(h, win_ref[:, d:2 * d], preferred_element_type=F32)
        vt = lax.dot_general(wvt_ref[...], h, (((1,), (1,)), ((), ())), preferred_element_type=F32)
        for p in range(n_pairs):
            sl = slice(p * V7X_LANES, (p + 1) * V7X_LANES)
            q_ref[0, p, rows, :] = q[:, sl].astype(BF16)
            k_ref[0, p, rows, :] = k[:, sl].astype(BF16)
            for j in range(sub // tk):
                vt_ref[0, p, r // tk + j] = vt[sl, j * tk:(j + 1) * tk].astype(BF16)

        f_logit = jnp.dot(h, wf_ref[...], preferred_element_type=F32) + bf_ref[...]
        log_f = jnp.minimum(f_logit, 0.0) - jnp.log1p(jnp.exp(-jnp.abs(f_logit)))
        lanes = lax.broadcasted_iota(jnp.int32, log_f.shape, 1)
        log_f = jnp.where(lanes < n_heads, log_f, 0.0)
        sums = jnp.dot(tri_ref[...], jnp.concatenate(_split3(log_f), axis=1), preferred_element_type=F32)
        cum = (carry_ref[V7X_SUBLANES - 1:] + sums[:, :V7X_LANES] + sums[:, V7X_LANES:2 * V7X_LANES]
               + sums[:, 2 * V7X_LANES:])
        carry_ref[...] = cum[sub - V7X_SUBLANES:]

        parts = jnp.concatenate(_split3(cum * LOG2E), axis=1)
        aug = jnp.dot(parts, pqk_ref[...], preferred_element_type=F32)
        augq_ref[rows, :] = (aug[:, :V7X_LANES] + oneq_ref[...]).astype(BF16)
        augk_ref[rows, :] = (aug[:, V7X_LANES:] + onek_ref[...]).astype(BF16)


def _aug_constants(n_heads):
    assert n_heads * AUG_PER_HEAD <= V7X_LANES
    pq = np.zeros((3 * V7X_LANES, V7X_LANES), np.float32)
    pk = np.zeros((3 * V7X_LANES, V7X_LANES), np.float32)
    oneq = np.zeros((1, V7X_LANES), np.float32)
    onek = np.zeros((1, V7X_LANES), np.float32)
    for h in range(n_heads):
        for j in range(3):
            pq[j * V7X_LANES + h, AUG_PER_HEAD * h + j] = 1.0
            pk[j * V7X_LANES + h, AUG_PER_HEAD * h + 3 + j] = -1.0
            oneq[0, AUG_PER_HEAD * h + 3 + j] = 1.0
            onek[0, AUG_PER_HEAD * h + j] = 1.0
    return jnp.asarray(np.concatenate([pq, pk], axis=1), BF16), jnp.asarray(oneq), jnp.asarray(onek)


def _fox_proj(x2, bsz, seq, shift, scale, g_pre, w_in_all, j, b_f):
    t, d = x2.shape
    n_heads = d // FOX_HEAD_DIM
    n_pairs = d // V7X_LANES
    tm = min(PROJ_ROW_TILE, seq)
    sub = min(ROW_TILE, tm)
    tiles_per_batch = seq // tm
    row, mod, vec = _row_specs(tm, d, tiles_per_batch)
    tk = min(ATTN_TILE, seq)
    assert tm % sub == 0 and sub % tk == 0
    w_qk = w_in_all[j, :, :2 * d]
    w_vt = w_in_all[j, :, 2 * d:3 * d].T
    w_f = jnp.pad(w_in_all[j, :, 3 * d:], ((0, 0), (0, V7X_LANES - n_heads)))
    b_f = jnp.pad(b_f, (0, V7X_LANES - n_heads)).reshape(1, V7X_LANES)
    tri = jnp.asarray(np.tril(np.ones((sub, sub), np.float32)), BF16)
    pqk, oneq, onek = _aug_constants(n_heads)
    pair_spec = pl.BlockSpec((1, n_pairs, tm, V7X_LANES),
                             lambda i: (i // tiles_per_batch, 0, i % tiles_per_batch, 0))
    vt_spec = pl.BlockSpec((1, n_pairs, tm // tk, V7X_LANES, tk),
                           lambda i: (i // tiles_per_batch, 0, i % tiles_per_batch, 0, 0))
    aug_spec = pl.BlockSpec((tm, V7X_LANES), lambda i: (i, 0))
    pair_shape = jax.ShapeDtypeStruct((bsz, n_pairs, seq, V7X_LANES), BF16)
    vt_shape = jax.ShapeDtypeStruct((bsz, n_pairs, seq // tk, V7X_LANES, tk), BF16)
    aug_shape = jax.ShapeDtypeStruct((t, V7X_LANES), BF16)
    return pl.pallas_call(
        functools.partial(_fox_proj_kernel, tiles_per_batch, n_heads),
        grid=(t // tm,),
        in_specs=[row, mod, mod, vec, _resident(w_qk.shape), _resident(w_vt.shape), _resident(w_f.shape),
                  _resident(b_f.shape), _resident(tri.shape), _resident(pqk.shape),
                  _resident(oneq.shape), _resident(onek.shape)],
        out_specs=[pair_spec, pair_spec, vt_spec, aug_spec, aug_spec],
        out_shape=[pair_shape, pair_shape, vt_shape, aug_shape, aug_shape],
        scratch_shapes=[pltpu.VMEM((V7X_SUBLANES, V7X_LANES), F32)],
        compiler_params=_params(1),
        name="fox_proj",
    )(x2, shift, scale, g_pre, w_qk, w_vt, w_f, b_f, tri, pqk, oneq, onek)


def _fox_attn_kernel(group_pairs, x_ref, gate_ref, gpost_ref, q_ref, k_ref, vt_ref, augq_ref, augk_ref,
                     wout_ref, o_ref, lhs_ref, stage_ref, mx_ref, alpha_ref, acc_ref, heads_ref):
    tq = x_ref.shape[0]
    n_pairs = q_ref.shape[1]
    qi = pl.program_id(1)
    lanes = lax.broadcasted_iota(jnp.int32, (tq, V7X_LANES), 1)
    half = lanes // FOX_HEAD_DIM
    causal_t = (lax.broadcasted_iota(jnp.int32, (tq, tq), 0)
                <= lax.broadcasted_iota(jnp.int32, (tq, tq), 1))
    augq = augq_ref[...]
    contract_last = (((1,), (1,)), ((), ()))
    zero16 = jnp.zeros((), BF16)
    ones_rows = jnp.ones((FOX_HEAD_DIM, tq), BF16)
    n_slabs = tq // V7X_SUBLANES

    def key_rows(kb):
        return pl.ds(pl.multiple_of(kb * tq, tq), tq)

    def over_rows(vec, n_rows):
        return jnp.concatenate([vec] * (n_rows // V7X_SUBLANES), axis=0)

    def stage(pairs):
        for p in pairs:
            q_pair = q_ref[0, p]
            for e in range(2):
                h = 2 * p + e
                own_aug = (lanes >= AUG_PER_HEAD * h) & (lanes < AUG_PER_HEAD * (h + 1))
                lhs_ref[h] = jnp.concatenate(
                    [jnp.where(half == e, q_pair, zero16), jnp.where(own_aug, augq, zero16)], axis=1)
                mx_ref[h] = jnp.full((V7X_SUBLANES, tq), -jnp.inf, F32)
                acc_ref[h] = jnp.zeros((V7X_LANES, tq), F32)

    def attend(pairs, kb, n_blocks, masked, slot):
        ones = jnp.concatenate([ones_rows] * n_blocks, axis=1)
        for p in pairs:
            keys = [jnp.concatenate([k_ref[0, p, key_rows(kb + j), :], augk_ref[key_rows(kb + j), :]], axis=1)
                    for j in range(n_blocks)]
            for e in range(2):
                h = 2 * p + e
                hh = h - 2 * pairs[0]
                slabs = []
                for j in range(n_blocks):
                    s = lax.dot_general(keys[j], lhs_ref[h], contract_last, preferred_element_type=F32)
                    if masked:
                        s = jnp.where(causal_t, s, -jnp.inf)
                    stage_ref[slot, hh, j] = s
                    slabs += [s[i * V7X_SUBLANES:(i + 1) * V7X_SUBLANES] for i in range(n_slabs)]
                while len(slabs) > 1:
                    slabs = [jnp.maximum(a, b) for a, b in zip(slabs[0::2], slabs[1::2])]
                m_old = mx_ref[h]
                m_new = jnp.maximum(m_old, jnp.broadcast_to(jnp.max(slabs[0], axis=0, keepdims=True),
                                                            (V7X_SUBLANES, tq)))
                mx_ref[h] = m_new
                alpha_ref[h] = jnp.exp2(m_old - m_new)
        for p in pairs:
            vt_pair = jnp.concatenate([vt_ref[0, p, kb + j] for j in range(n_blocks)], axis=1)
            for e in range(2):
                h = 2 * p + e
                hh = h - 2 * pairs[0]
                m_rows = over_rows(mx_ref[h], tq)
                prob = jnp.concatenate(
                    [jnp.exp2(stage_ref[slot, hh, j] - m_rows).astype(BF16) for j in range(n_blocks)], axis=0)
                vt_own = (jnp.concatenate([vt_pair[:FOX_HEAD_DIM], ones], axis=0) if e == 0 else
                          jnp.concatenate([ones, vt_pair[FOX_HEAD_DIM:]], axis=0))
                acc_ref[h] = (over_rows(alpha_ref[h], V7X_LANES) * acc_ref[h]
                              + jnp.dot(vt_own, prob, preferred_element_type=F32))

    def finalize(pairs):
        for p in pairs:
            a_even = acc_ref[2 * p]
            a_odd = acc_ref[2 * p + 1]
            numer = jnp.concatenate([a_even[:FOX_HEAD_DIM], a_odd[FOX_HEAD_DIM:]], axis=0)
            denom = jnp.concatenate([a_even[FOX_HEAD_DIM:], a_odd[:FOX_HEAD_DIM]], axis=0)
            heads_ref[p] = (numer / denom).T.astype(BF16)

    for g in range(n_pairs // group_pairs):
        pairs = range(g * group_pairs, (g + 1) * group_pairs)
        stage(pairs)
        attend(pairs, qi, 1, True, qi % 2)

        def body(i, c, pairs=pairs):
            attend(pairs, 2 * i, 2, False, i % 2)
            return c

        lax.fori_loop(0, qi // 2, body, 0)

        @pl.when(qi % 2 == 1)
        def _(pairs=pairs):
            attend(pairs, qi - 1, 1, False, qi % 2)

        finalize(pairs)

    o = jnp.concatenate([heads_ref[p] for p in range(n_pairs)], axis=1)
    y = jnp.dot(o.astype(F32), wout_ref[...], preferred_element_type=F32)
    o_ref[...] = _residual(x_ref[...], y, gate_ref, gpost_ref)


def _fox_attn(x2, bsz, seq, gate, g_post, q, k, vt, augq, augk, w_out_all, j):
    t, d = x2.shape
    n_pairs = d // V7X_LANES
    tq = min(ATTN_TILE, seq)
    nq = seq // tq
    group_pairs = min(ATTN_GROUP_PAIRS, n_pairs)
    assert n_pairs % group_pairs == 0
    row = pl.BlockSpec((tq, d), lambda b, i: (b * nq + i, 0))
    mod = pl.BlockSpec((1, 1, d), lambda b, i: (b, 0, 0))
    vec = pl.BlockSpec((1, d), lambda b, i: (0, 0))
    q_spec = pl.BlockSpec((1, n_pairs, tq, V7X_LANES), lambda b, i: (b, 0, i, 0))
    k_spec = pl.BlockSpec((1, n_pairs, seq, V7X_LANES), lambda b, i: (b, 0, 0, 0))
    vt_spec = pl.BlockSpec((1, n_pairs, nq, V7X_LANES, tq), lambda b, i: (b, 0, 0, 0, 0))
    augq_spec = pl.BlockSpec((tq, V7X_LANES), lambda b, i: (b * nq + i, 0))
    augk_spec = pl.BlockSpec((seq, V7X_LANES), lambda b, i: (b, 0))
    return pl.pallas_call(
        functools.partial(_fox_attn_kernel, group_pairs),
        grid=(bsz, nq),
        in_specs=[row, mod, vec, q_spec, k_spec, vt_spec, augq_spec, augk_spec,
                  _stacked_resident(w_out_all, (j,))],
        out_specs=row,
        out_shape=jax.ShapeDtypeStruct((t, d), F32),
        scratch_shapes=[pltpu.VMEM((2 * n_pairs, tq, 2 * V7X_LANES), BF16),
                        pltpu.VMEM((2, 2 * group_pairs, 2, tq, tq), F32),
                        pltpu.VMEM((2 * n_pairs, V7X_SUBLANES, tq), F32),
                        pltpu.VMEM((2 * n_pairs, V7X_SUBLANES, tq), F32),
                        pltpu.VMEM((2 * n_pairs, V7X_LANES, tq), F32),
                        pltpu.VMEM((n_pairs, tq, V7X_LANES), BF16)],
        compiler_params=_params(2),
        name="fox_attn",
    )(x2, gate, g_post, q, k, vt, augq, augk, w_out_all)


def kernel(x, c, w_cond, b_cond, norm_pre, norm_post, w_ffn_in, w_ffn_out, fox_w_in, fox_b_f, fox_w_out, sconv_w_in, sconv_conv_w, sconv_w_out, lru_w_in, lru_conv_w, lru_conv_b, lru_w_a, lru_b_a, lru_w_x, lru_b_x, lru_lambda, lru_w_out):
    bsz, seq, d = x.shape
    depth = w_cond.shape[0]
    assert d % V7X_LANES == 0 and V7X_LANES == 2 * FOX_HEAD_DIM
    assert all(seq % min(tile, seq) == 0 for tile in (ROW_TILE, PROJ_ROW_TILE, FFN_ROW_TILE, ATTN_TILE))

    mod = _modulation(c, w_cond, b_cond).reshape(depth, bsz, N_SUB, 3, 1, d)
    x2 = x.reshape(bsz * seq, d)

    for i in range(depth):
        def mods(s):
            return mod[i, :, s, 0], mod[i, :, s, 1], mod[i, :, s, 2]

        def gains(s):
            return norm_pre[i, s].reshape(1, d), norm_post[i, s].reshape(1, d)

        shift, scale, gate = mods(0)
        x2 = _ffn(x2, seq, shift, scale, gate, *gains(0), w_ffn_in, w_ffn_out, (i, 0))

        shift, scale, gate = mods(1)
        g_pre, g_post = gains(1)
        kind, j = i % N_MIXERS, i // N_MIXERS
        if kind == 0:
            q, k, vt, augq, augk = _fox_proj(x2, bsz, seq, shift, scale, g_pre, fox_w_in, j, fox_b_f[j])
            x2 = _fox_attn(x2, bsz, seq, gate, g_post, q, k, vt, augq, augk, fox_w_out, j)
        elif kind == 1:
            x2 = _sconv(x2, seq, shift, scale, gate, g_pre, g_post, sconv_w_in[j], sconv_conv_w[j],
                        sconv_w_out[j])
        else:
            x2 = _lru(x2, seq, shift, scale, gate, g_pre, g_post, lru_w_in[j], lru_conv_w[j],
                      lru_conv_b[j], lru_w_a[j], lru_b_a[j], lru_w_x[j], lru_b_x[j], lru_lambda[j],
                      lru_w_out[j])

        shift, scale, gate = mods(2)
        x2 = _ffn(x2, seq, shift, scale, gate, *gains(2), w_ffn_in, w_ffn_out, (i, 1))

    return x2.reshape(bsz, seq, d)
```

```python
import functools

import numpy as np
import jax
import jax.numpy as jnp
from jax import lax
from jax.experimental import pallas as pl
from jax.experimental.pallas import tpu as pltpu

N_SUB = 3
N_MIXERS = 3
FFN_RES_WEIGHT = 0.5
RMS_EPS = 1e-6
FOX_HEAD_DIM = 64
LRU_C = 8.0
LOG2E = 1.4426950408889634

V7X_LANES = 128
V7X_SUBLANES = 8
V7X_MXU_DIM = 256
V7X_VMEM_BYTES = 64 * 1024 * 1024
VMEM_LIMIT_BYTES = V7X_VMEM_BYTES - 3 * 1024 * 1024

ROW_TILE = 512
PROJ_ROW_TILE = 1024
ATTN_TILE = 256
FFN_CHUNK = V7X_MXU_DIM
FFN_ROW_TILE = 1024
FFN_SUBTILE = 512
AUG_PER_HEAD = 6
ATTN_GROUP_PAIRS = 8
MOD_COL_TILE = 1536

BF16 = jnp.bfloat16
F32 = jnp.float32


def _params(n_axes):
    return pltpu.CompilerParams(dimension_semantics=("arbitrary",) * n_axes,
                                vmem_limit_bytes=VMEM_LIMIT_BYTES)


def _resident(shape):
    return pl.BlockSpec(shape, lambda *_: (0,) * len(shape), pipeline_mode=pl.Buffered(1))


def _rms(x):
    return x * lax.rsqrt(jnp.mean(x * x, axis=-1, keepdims=True) + RMS_EPS)


def _pre(x, gpre_ref, scale_ref, shift_ref):
    return _rms(x) * (gpre_ref[...] * (1.0 + scale_ref[0])) + shift_ref[0]


def _residual(x, y, gate_ref, gpost_ref, weight=1.0):
    return x + _rms(y) * ((weight * gate_ref[0]) * gpost_ref[...])


def _split3(x):
    hi = x.astype(BF16)
    r1 = x - hi.astype(F32)
    mid = r1.astype(BF16)
    lo = (r1 - mid.astype(F32)).astype(BF16)
    return hi, mid, lo


def _mod_kernel(c_ref, w_ref, b_ref, o_ref):
    c = c_ref[...]
    c_act = (c * jax.nn.sigmoid(c)).astype(BF16)
    o_ref[0] = jnp.dot(c_act, w_ref[0].astype(BF16), preferred_element_type=F32) + b_ref[0]


def _modulation(c, w_cond, b_cond):
    depth, d, n = w_cond.shape
    bsz = c.shape[0]
    tn = MOD_COL_TILE if n % MOD_COL_TILE == 0 else n
    return pl.pallas_call(
        _mod_kernel,
        grid=(depth, n // tn),
        in_specs=[
            pl.BlockSpec((bsz, d), lambda i, j: (0, 0)),
            pl.BlockSpec((1, d, tn), lambda i, j: (i, 0, j)),
            pl.BlockSpec((1, 1, tn), lambda i, j: (i, 0, j)),
        ],
        out_specs=pl.BlockSpec((1, bsz, tn), lambda i, j: (i, 0, j)),
        out_shape=jax.ShapeDtypeStruct((depth, bsz, n), F32),
        compiler_params=_params(2),
        name="adaln_mod",
    )(c, w_cond, b_cond.reshape(depth, 1, n))


def _ffn_kernel(x_ref, shift_ref, scale_ref, gate_ref, gpre_ref, gpost_ref, win_ref, wout_ref,
                o_ref, act_ref):
    f = wout_ref.shape[0]
    fc = FFN_CHUNK
    tm = x_ref.shape[0]
    for r in range(0, tm, FFN_SUBTILE):
        rows = slice(r, r + FFN_SUBTILE)
        x = x_ref[rows, :]
        h = _pre(x, gpre_ref, scale_ref, shift_ref)
        for c in range(f // fc):
            g = jnp.dot(h, win_ref[:, c * fc:(c + 1) * fc], preferred_element_type=F32)
            u = jnp.dot(h, win_ref[:, f + c * fc:f + (c + 1) * fc], preferred_element_type=F32)
            act_ref[rows, c * fc:(c + 1) * fc] = (g * jax.nn.sigmoid(g) * u).astype(BF16)
        y = jnp.dot(act_ref[rows, :].astype(F32), wout_ref[...], preferred_element_type=F32)
        o_ref[rows, :] = _residual(x, y, gate_ref, gpost_ref, FFN_RES_WEIGHT)


def _row_specs(tm, d, tiles_per_batch):
    row = pl.BlockSpec((tm, d), lambda i: (i, 0))
    mod = pl.BlockSpec((1, 1, d), lambda i: (i // tiles_per_batch, 0, 0))
    vec = pl.BlockSpec((1, d), lambda i: (0, 0))
    return row, mod, vec


def _stacked_resident(stacked, index):
    lead = len(index)
    shape = (None,) * lead + tuple(stacked.shape[lead:])
    return pl.BlockSpec(shape, lambda *_: tuple(index) + (0,) * (stacked.ndim - lead),
                        pipeline_mode=pl.Buffered(1))


def _ffn(x2, seq, shift, scale, gate, g_pre, g_post, w_in, w_out, index):
    t, d = x2.shape
    tm = min(FFN_ROW_TILE, seq)
    f = w_out.shape[-2]
    assert f % FFN_CHUNK == 0 and tm % FFN_SUBTILE == 0
    row, mod, vec = _row_specs(tm, d, seq // tm)
    return pl.pallas_call(
        _ffn_kernel,
        grid=(t // tm,),
        in_specs=[row, mod, mod, mod, vec, vec, _stacked_resident(w_in, index),
                  _stacked_resident(w_out, index)],
        out_specs=row,
        out_shape=jax.ShapeDtypeStruct((t, d), F32),
        scratch_shapes=[pltpu.VMEM((tm, f), BF16)],
        compiler_params=_params(1),
        name="ffn",
    )(x2, shift, scale, gate, g_pre, g_post, w_in, w_out)


def _causal_conv(u, hist_ref, w_ref):
    tm = u.shape[0]
    kw = w_ref.shape[0]
    hist_ref[V7X_SUBLANES:, :] = u
    out = w_ref[kw - 1:kw] * u
    for k in range(1, kw):
        out = out + w_ref[kw - 1 - k:kw - k] * hist_ref[V7X_SUBLANES - k:V7X_SUBLANES - k + tm, :]
    hist_ref[:V7X_SUBLANES, :] = u[tm - V7X_SUBLANES:]
    return out


def _sconv_kernel(tiles_per_batch, x_ref, shift_ref, scale_ref, gate_ref, gpre_ref, gpost_ref,
                  win_ref, convw_ref, wout_ref, o_ref, hist_ref):
    d = x_ref.shape[1]

    @pl.when(pl.program_id(0) % tiles_per_batch == 0)
    def _():
        hist_ref[:V7X_SUBLANES, :] = jnp.zeros((V7X_SUBLANES, d), F32)

    sub = hist_ref.shape[0] - V7X_SUBLANES
    for r in range(0, x_ref.shape[0], sub):
        rows = slice(r, r + sub)
        x = x_ref[rows, :]
        h = _pre(x, gpre_ref, scale_ref, shift_ref)
        c_gate = jnp.dot(h, win_ref[:, d:2 * d], preferred_element_type=F32)
        xv = jnp.dot(h, win_ref[:, 2 * d:], preferred_element_type=F32)
        u = c_gate * xv
        conv = _causal_conv(u, hist_ref, convw_ref)
        b_gate = jnp.dot(h, win_ref[:, :d], preferred_element_type=F32)
        y = jnp.dot(b_gate * conv, wout_ref[...], preferred_element_type=F32)
        o_ref[rows, :] = _residual(x, y, gate_ref, gpost_ref)


def _sconv(x2, seq, shift, scale, gate, g_pre, g_post, w_in, conv_w, w_out):
    t, d = x2.shape
    tm = min(PROJ_ROW_TILE, seq)
    sub = min(ROW_TILE, tm)
    assert tm % sub == 0
    row, mod, vec = _row_specs(tm, d, seq // tm)
    return pl.pallas_call(
        functools.partial(_sconv_kernel, seq // tm),
        grid=(t // tm,),
        in_specs=[row, mod, mod, mod, vec, vec, _resident(w_in.shape), _resident(conv_w.shape),
                  _resident(w_out.shape)],
        out_specs=row,
        out_shape=jax.ShapeDtypeStruct((t, d), F32),
        scratch_shapes=[pltpu.VMEM((V7X_SUBLANES + sub, d), F32)],
        compiler_params=_params(1),
        name="sconv_mixer",
    )(x2, shift, scale, gate, g_pre, g_post, w_in, conv_w, w_out)


def _linear_scan(a, b, h_prev):
    n, c = a.shape
    grouped = (n // V7X_SUBLANES, V7X_SUBLANES, c)
    sub = lax.broadcasted_iota(jnp.int32, grouped, 1)
    a = a.reshape(grouped)
    b = b.reshape(grouped)
    d = 1
    while d < V7X_SUBLANES:
        keep = sub >= d
        b = jnp.where(keep, b + a * pltpu.roll(b, d, axis=1), b)
        a = jnp.where(keep, a * pltpu.roll(a, d, axis=1), a)
        d *= 2
    a = a.reshape(n, c)
    b = b.reshape(n, c)
    groups = []
    for r in range(0, n, V7X_SUBLANES):
        h_group = a[r:r + V7X_SUBLANES] * h_prev + b[r:r + V7X_SUBLANES]
        groups.append(h_group)
        h_prev = h_group[V7X_SUBLANES - 1:]
    return jnp.concatenate(groups, axis=0)


def _lru_kernel(tiles_per_batch, x_ref, shift_ref, scale_ref, gate_ref, gpre_ref, gpost_ref,
                win_ref, convw_ref, convb_ref, wgate_ref, ba_ref, bx_ref, lam_ref, wout_ref,
                o_ref, xhist_ref, hcarry_ref):
    tm = x_ref.shape[0]
    w = wout_ref.shape[0]
    n_groups, gk, two_gn = wgate_ref.shape
    gn = two_gn // 2

    @pl.when(pl.program_id(0) % tiles_per_batch == 0)
    def _():
        xhist_ref[:V7X_SUBLANES, :] = jnp.zeros((V7X_SUBLANES, w), F32)
        hcarry_ref[...] = jnp.zeros(hcarry_ref.shape, F32)

    x = x_ref[...]
    h = _pre(x, gpre_ref, scale_ref, shift_ref).astype(BF16)
    xraw = jnp.dot(h, win_ref[:, w:], preferred_element_type=F32)
    xb = _causal_conv(xraw, xhist_ref, convw_ref) + convb_ref[...]

    xb16 = xb.astype(BF16)
    r_parts, i_parts = [], []
    for g in range(n_groups):
        ri = jnp.dot(xb16[:, g * gk:(g + 1) * gk], wgate_ref[g], preferred_element_type=F32)
        r_parts.append(ri[:, :gn])
        i_parts.append(ri[:, gn:])
    r = jax.nn.sigmoid(jnp.concatenate(r_parts, axis=1) + ba_ref[...])
    i = jax.nn.sigmoid(jnp.concatenate(i_parts, axis=1) + bx_ref[...])

    neg_lam = -lam_ref[...]
    softplus = jnp.maximum(neg_lam, 0.0) + jnp.log1p(jnp.exp(-jnp.abs(neg_lam)))
    log_a = (-LRU_C * softplus) * r
    a = jnp.exp(log_a)
    mult = jnp.sqrt(-jnp.tanh(log_a) * (1.0 + a * a))
    hs = _linear_scan(a, mult * (i * xb), hcarry_ref[V7X_SUBLANES - 1:])
    hcarry_ref[...] = hs[tm - V7X_SUBLANES:]

    gate_br = jnp.dot(h, win_ref[:, :w], preferred_element_type=F32)
    y = jnp.dot((hs * jax.nn.gelu(gate_br)).astype(BF16), wout_ref[...], preferred_element_type=F32)
    o_ref[...] = _residual(x, y, gate_ref, gpost_ref)


def _lru_gate_weights(w_a, w_x):
    nb, bd, _ = w_a.shape
    per = V7X_MXU_DIM // bd
    assert nb % per == 0

    def dense(wb):
        wg = wb.reshape(nb // per, per, bd, bd)
        eye = jnp.eye(per, dtype=wb.dtype)
        return jnp.einsum("gpij,pq->gpiqj", wg, eye).reshape(nb // per, per * bd, per * bd)

    return jnp.concatenate([dense(w_a), dense(w_x)], axis=-1).astype(BF16)


def _lru(x2, seq, shift, scale, gate, g_pre, g_post, w_in, conv_w, conv_b, w_a, b_a, w_x, b_x, lam,
         w_out):
    t, d = x2.shape
    w = w_out.shape[0]
    tm = min(ROW_TILE, seq)
    row, mod, vec = _row_specs(tm, d, seq // tm)
    wvec = pl.BlockSpec((1, w), lambda i: (0, 0))
    w_gate = _lru_gate_weights(w_a, w_x)
    return pl.pallas_call(
        functools.partial(_lru_kernel, seq // tm),
        grid=(t // tm,),
        in_specs=[row, mod, mod, mod, vec, vec, _resident(w_in.shape), _resident(conv_w.shape), wvec,
                  _resident(w_gate.shape), wvec, wvec, wvec, _resident(w_out.shape)],
        out_specs=row,
        out_shape=jax.ShapeDtypeStruct((t, d), F32),
        scratch_shapes=[pltpu.VMEM((V7X_SUBLANES + tm, w), F32), pltpu.VMEM((V7X_SUBLANES, w), F32)],
        compiler_params=_params(1),
        name="lru_mixer",
    )(x2, shift, scale, gate, g_pre, g_post, w_in.astype(BF16), conv_w, conv_b.reshape(1, w), w_gate,
      b_a.reshape(1, w), b_x.reshape(1, w), lam.reshape(1, w), w_out.astype(BF16))


def _fox_proj_kernel(tiles_per_batch, n_heads, x_ref, shift_ref, scale_ref, gpre_ref, win_ref, wvt_ref, wf_ref,
                     bf_ref, tri_ref, pqk_ref, oneq_ref, onek_ref,
                     q_ref, k_ref, vt_ref, augq_ref, augk_ref, carry_ref):
    d = x_ref.shape[1]
    n_pairs = q_ref.shape[1]
    tk = vt_ref.shape[4]

    @pl.when(pl.program_id(0) % tiles_per_batch == 0)
    def _():
        carry_ref[...] = jnp.zeros(carry_ref.shape, F32)

    sub = tri_ref.shape[0]
    for r in range(0, x_ref.shape[0], sub):
        rows = slice(r, r + sub)
        h = _pre(x_ref[rows, :], gpre_ref, scale_ref, shift_ref)
        q = jnp.dot(h, win_ref[:, :d], preferred_element_type=F32) * (FOX_HEAD_DIM ** -0.5 * LOG2E)
        k = jnp.dot(h, win_ref[:, d:2 * d], preferred_element_type=F32)
        vt = lax.dot_general(wvt_ref[...], h, (((1,), (1,)), ((), ())), preferred_element_type=F32)
        for p in range(n_pairs):
            sl = slice(p * V7X_LANES, (p + 1) * V7X_LANES)
            q_ref[0, p, rows, :] = q[:, sl].astype(BF16)
            k_ref[0, p, rows, :] = k[:, sl].astype(BF16)
            for j in range(sub // tk):
                vt_ref[0, p, r // tk + j] = vt[sl, j * tk:(j + 1) * tk].astype(BF16)

        f_logit = jnp.dot(h, wf_ref[...], preferred_element_type=F32) + bf_ref[...]
        log_f = jnp.minimum(f_logit, 0.0) - jnp.log1p(jnp.exp(-jnp.abs(f_logit)))
        lanes = lax.broadcasted_iota(jnp.int32, log_f.shape, 1)
        log_f = jnp.where(lanes < n_heads, log_f, 0.0)
        sums = jnp.dot(tri_ref[...], jnp.concatenate(_split3(log_f), axis=1), preferred_element_type=F32)
        cum = (carry_ref[V7X_SUBLANES - 1:] + sums[:, :V7X_LANES] + sums[:, V7X_LANES:2 * V7X_LANES]
               + sums[:, 2 * V7X_LANES:])
        carry_ref[...] = cum[sub - V7X_SUBLANES:]

        parts = jnp.concatenate(_split3(cum * LOG2E), axis=1)
        aug = jnp.dot(parts, pqk_ref[...], preferred_element_type=F32)
        augq_ref[rows, :] = (aug[:, :V7X_LANES] + oneq_ref[...]).astype(BF16)
        augk_ref[rows, :] = (aug[:, V7X_LANES:] + onek_ref[...]).astype(BF16)


def _aug_constants(n_heads):
    assert n_heads * AUG_PER_HEAD <= V7X_LANES
    pq = np.zeros((3 * V7X_LANES, V7X_LANES), np.float32)
    pk = np.zeros((3 * V7X_LANES, V7X_LANES), np.float32)
    oneq = np.zeros((1, V7X_LANES), np.float32)
    onek = np.zeros((1, V7X_LANES), np.float32)
    for h in range(n_heads):
        for j in range(3):
            pq[j * V7X_LANES + h, AUG_PER_HEAD * h + j] = 1.0
            pk[j * V7X_LANES + h, AUG_PER_HEAD * h + 3 + j] = -1.0
            oneq[0, AUG_PER_HEAD * h + 3 + j] = 1.0
            onek[0, AUG_PER_HEAD * h + j] = 1.0
    return jnp.asarray(np.concatenate([pq, pk], axis=1), BF16), jnp.asarray(oneq), jnp.asarray(onek)


def _fox_proj(x2, bsz, seq, shift, scale, g_pre, w_in_all, j, b_f):
    t, d = x2.shape
    n_heads = d // FOX_HEAD_DIM
    n_pairs = d // V7X_LANES
    tm = min(PROJ_ROW_TILE, seq)
    sub = min(ROW_TILE, tm)
    tiles_per_batch = seq // tm
    row, mod, vec = _row_specs(tm, d, tiles_per_batch)
    tk = min(ATTN_TILE, seq)
    assert tm % sub == 0 and sub % tk == 0
    w_qk = w_in_all[j, :, :2 * d]
    w_vt = w_in_all[j, :, 2 * d:3 * d].T
    w_f = jnp.pad(w_in_all[j, :, 3 * d:], ((0, 0), (0, V7X_LANES - n_heads)))
    b_f = jnp.pad(b_f, (0, V7X_LANES - n_heads)).reshape(1, V7X_LANES)
    tri = jnp.asarray(np.tril(np.ones((sub, sub), np.float32)), BF16)
    pqk, oneq, onek = _aug_constants(n_heads)
    pair_spec = pl.BlockSpec((1, n_pairs, tm, V7X_LANES),
                             lambda i: (i // tiles_per_batch, 0, i % tiles_per_batch, 0))
    vt_spec = pl.BlockSpec((1, n_pairs, tm // tk, V7X_LANES, tk),
                           lambda i: (i // tiles_per_batch, 0, i % tiles_per_batch, 0, 0))
    aug_spec = pl.BlockSpec((tm, V7X_LANES), lambda i: (i, 0))
    pair_shape = jax.ShapeDtypeStruct((bsz, n_pairs, seq, V7X_LANES), BF16)
    vt_shape = jax.ShapeDtypeStruct((bsz, n_pairs, seq // tk, V7X_LANES, tk), BF16)
    aug_shape = jax.ShapeDtypeStruct((t, V7X_LANES), BF16)
    return pl.pallas_call(
        functools.partial(_fox_proj_kernel, tiles_per_batch, n_heads),
        grid=(t // tm,),
        in_specs=[row, mod, mod, vec, _resident(w_qk.shape), _resident(w_vt.shape), _resident(w_f.shape),
                  _resident(b_f.shape), _resident(tri.shape), _resident(pqk.shape),
                  _resident(oneq.shape), _resident(onek.shape)],
        out_specs=[pair_spec, pair_spec, vt_spec, aug_spec, aug_spec],
        out_shape=[pair_shape, pair_shape, vt_shape, aug_shape, aug_shape],
        scratch_shapes=[pltpu.VMEM((V7X_SUBLANES, V7X_LANES), F32)],
        compiler_params=_params(1),
        name="fox_proj",
    )(x2, shift, scale, g_pre, w_qk, w_vt, w_f, b_f, tri, pqk, oneq, onek)


def _fox_attn_kernel(group_pairs, x_ref, gate_ref, gpost_ref, q_ref, k_ref, vt_ref, augq_ref, augk_ref,
                     wout_ref, o_ref, lhs_ref, stage_ref, mx_ref, alpha_ref, acc_ref, heads_ref):
    tq = x_ref.shape[0]
    n_pairs = q_ref.shape[1]
    qi = pl.program_id(1)
    lanes = lax.broadcasted_iota(jnp.int32, (tq, V7X_LANES), 1)
    half = lanes // FOX_HEAD_DIM
    causal_t = (lax.broadcasted_iota(jnp.int32, (tq, tq), 0)
                <= lax.broadcasted_iota(jnp.int32, (tq, tq), 1))
    augq = augq_ref[...]
    contract_last = (((1,), (1,)), ((), ()))
    zero16 = jnp.zeros((), BF16)
    ones_rows = jnp.ones((FOX_HEAD_DIM, tq), BF16)
    n_slabs = tq // V7X_SUBLANES

    def key_rows(kb):
        return pl.ds(pl.multiple_of(kb * tq, tq), tq)

    def over_rows(vec, n_rows):
        return jnp.concatenate([vec] * (n_rows // V7X_SUBLANES), axis=0)

    def stage(pairs):
        for p in pairs:
            q_pair = q_ref[0, p]
            for e in range(2):
                h = 2 * p + e
                own_aug = (lanes >= AUG_PER_HEAD * h) & (lanes < AUG_PER_HEAD * (h + 1))
                lhs_ref[h] = jnp.concatenate(
                    [jnp.where(half == e, q_pair, zero16), jnp.where(own_aug, augq, zero16)], axis=1)
                mx_ref[h] = jnp.full((V7X_SUBLANES, tq), -jnp.inf, F32)
                acc_ref[h] = jnp.zeros((V7X_LANES, tq), F32)

    def attend(pairs, kb, n_blocks, masked, slot):
        ones = jnp.concatenate([ones_rows] * n_blocks, axis=1)
        for p in pairs:
            keys = [jnp.concatenate([k_ref[0, p, key_rows(kb + j), :], augk_ref[key_rows(kb + j), :]], axis=1)
                    for j in range(n_blocks)]
            for e in range(2):
                h = 2 * p + e
                hh = h - 2 * pairs[0]
                slabs = []
                for j in range(n_blocks):
                    s = lax.dot_general(keys[j], lhs_ref[h], contract_last, preferred_element_type=F32)
                    if masked:
                        s = jnp.where(causal_t, s, -jnp.inf)
                    stage_ref[slot, hh, j] = s
                    slabs += [s[i * V7X_SUBLANES:(i + 1) * V7X_SUBLANES] for i in range(n_slabs)]
                while len(slabs) > 1:
                    slabs = [jnp.maximum(a, b) for a, b in zip(slabs[0::2], slabs[1::2])]
                m_old = mx_ref[h]
                m_new = jnp.maximum(m_old, jnp.broadcast_to(jnp.max(slabs[0], axis=0, keepdims=True),
                                                            (V7X_SUBLANES, tq)))
                mx_ref[h] = m_new
                alpha_ref[h] = jnp.exp2(m_old - m_new)
        for p in pairs:
            vt_pair = jnp.concatenate([vt_ref[0, p, kb + j] for j in range(n_blocks)], axis=1)
            for e in range(2):
                h = 2 * p + e
                hh = h - 2 * pairs[0]
                m_rows = over_rows(mx_ref[h], tq)
                prob = jnp.concatenate(
                    [jnp.exp2(stage_ref[slot, hh, j] - m_rows).astype(BF16) for j in range(n_blocks)], axis=0)
                vt_own = (jnp.concatenate([vt_pair[:FOX_HEAD_DIM], ones], axis=0) if e == 0 else
                          jnp.concatenate([ones, vt_pair[FOX_HEAD_DIM:]], axis=0))
                acc_ref[h] = (over_rows(alpha_ref[h], V7X_LANES) * acc_ref[h]
                              + jnp.dot(vt_own, prob, preferred_element_type=F32))

    def finalize(pairs):
        for p in pairs:
            a_even = acc_ref[2 * p]
            a_odd = acc_ref[2 * p + 1]
            numer = jnp.concatenate([a_even[:FOX_HEAD_DIM], a_odd[FOX_HEAD_DIM:]], axis=0)
            denom = jnp.concatenate([a_even[FOX_HEAD_DIM:], a_odd[:FOX_HEAD_DIM]], axis=0)
            heads_ref[p] = (numer / denom).T.astype(BF16)

    for g in range(n_pairs // group_pairs):
        pairs = range(g * group_pairs, (g + 1) * group_pairs)
        stage(pairs)
        attend(pairs, qi, 1, True, qi % 2)

        def body(i, c, pairs=pairs):
            attend(pairs, 2 * i, 2, False, i % 2)
            return c

        lax.fori_loop(0, qi // 2, body, 0)

        @pl.when(qi % 2 == 1)
        def _(pairs=pairs):
            attend(pairs, qi - 1, 1, False, qi % 2)

        finalize(pairs)

    o = jnp.concatenate([heads_ref[p] for p in range(n_pairs)], axis=1)
    y = jnp.dot(o.astype(F32), wout_ref[...], preferred_element_type=F32)
    o_ref[...] = _residual(x_ref[...], y, gate_ref, gpost_ref)


def _fox_attn(x2, bsz, seq, gate, g_post, q, k, vt, augq, augk, w_out_all, j):
    t, d = x2.shape
    n_pairs = d // V7X_LANES
    tq = min(ATTN_TILE, seq)
    nq = seq // tq
    group_pairs = min(ATTN_GROUP_PAIRS, n_pairs)
    assert n_pairs % group_pairs == 0
    row = pl.BlockSpec((tq, d), lambda b, i: (b * nq + i, 0))
    mod = pl.BlockSpec((1, 1, d), lambda b, i: (b, 0, 0))
    vec = pl.BlockSpec((1, d), lambda b, i: (0, 0))
    q_spec = pl.BlockSpec((1, n_pairs, tq, V7X_LANES), lambda b, i: (b, 0, i, 0))
    k_spec = pl.BlockSpec((1, n_pairs, seq, V7X_LANES), lambda b, i: (b, 0, 0, 0))
    vt_spec = pl.BlockSpec((1, n_pairs, nq, V7X_LANES, tq), lambda b, i: (b, 0, 0, 0, 0))
    augq_spec = pl.BlockSpec((tq, V7X_LANES), lambda b, i: (b * nq + i, 0))
    augk_spec = pl.BlockSpec((seq, V7X_LANES), lambda b, i: (b, 0))
    return pl.pallas_call(
        functools.partial(_fox_attn_kernel, group_pairs),
        grid=(bsz, nq),
        in_specs=[row, mod, vec, q_spec, k_spec, vt_spec, augq_spec, augk_spec,
                  _stacked_resident(w_out_all, (j,))],
        out_specs=row,
        out_shape=jax.ShapeDtypeStruct((t, d), F32),
        scratch_shapes=[pltpu.VMEM((2 * n_pairs, tq, 2 * V7X_LANES), BF16),
                        pltpu.VMEM((2, 2 * group_pairs, 2, tq, tq), F32),
                        pltpu.VMEM((2 * n_pairs, V7X_SUBLANES, tq), F32),
                        pltpu.VMEM((2 * n_pairs, V7X_SUBLANES, tq), F32),
                        pltpu.VMEM((2 * n_pairs, V7X_LANES, tq), F32),
                        pltpu.VMEM((n_pairs, tq, V7X_LANES), BF16)],
        compiler_params=_params(2),
        name="fox_attn",
    )(x2, gate, g_post, q, k, vt, augq, augk, w_out_all)


def kernel(x, c, w_cond, b_cond, norm_pre, norm_post, w_ffn_in, w_ffn_out, fox_w_in, fox_b_f, fox_w_out, sconv_w_in, sconv_conv_w, sconv_w_out, lru_w_in, lru_conv_w, lru_conv_b, lru_w_a, lru_b_a, lru_w_x, lru_b_x, lru_lambda, lru_w_out):
    bsz, seq, d = x.shape
    depth = w_cond.shape[0]
    assert d % V7X_LANES == 0 and V7X_LANES == 2 * FOX_HEAD_DIM
    assert all(seq % min(tile, seq) == 0 for tile in (ROW_TILE, PROJ_ROW_TILE, FFN_ROW_TILE, ATTN_TILE))

    mod = _modulation(c, w_cond, b_cond).reshape(depth, bsz, N_SUB, 3, 1, d)
    x2 = x.reshape(bsz * seq, d)
    w_ffn_in16 = w_ffn_in
    w_ffn_out16 = w_ffn_out

    for i in range(depth):
        def mods(s):
            return mod[i, :, s, 0], mod[i, :, s, 1], mod[i, :, s, 2]

        def gains(s):
            return norm_pre[i, s].reshape(1, d), norm_post[i, s].reshape(1, d)

        shift, scale, gate = mods(0)
        x2 = _ffn(x2, seq, shift, scale, gate, *gains(0), w_ffn_in16, w_ffn_out16, (i, 0))

        shift, scale, gate = mods(1)
        g_pre, g_post = gains(1)
        kind, j = i % N_MIXERS, i // N_MIXERS
        if kind == 0:
            q, k, vt, augq, augk = _fox_proj(x2, bsz, seq, shift, scale, g_pre, fox_w_in, j, fox_b_f[j])
            x2 = _fox_attn(x2, bsz, seq, gate, g_post, q, k, vt, augq, augk, fox_w_out, j)
        elif kind == 1:
            x2 = _sconv(x2, seq, shift, scale, gate, g_pre, g_post, sconv_w_in[j], sconv_conv_w[j],
                        sconv_w_out[j])
        else:
            x2 = _lru(x2, seq, shift, scale, gate, g_pre, g_post, lru_w_in[j], lru_conv_w[j],
                      lru_conv_b[j], lru_w_a[j], lru_b_a[j], lru_w_x[j], lru_b_x[j], lru_lambda[j],
                      lru_w_out[j])

        shift, scale, gate = mods(2)
        x2 = _ffn(x2, seq, shift, scale, gate, *gains(2), w_ffn_in16, w_ffn_out16, (i, 1))

    return x2.reshape(bsz, seq, d)
```
